```python
import jax, jax.numpy as jnp
from jax import lax
import numpy as np

D_MODEL = 2048
BATCH = 4
SEQ = 2048
DEPTH = 4
DEC_BATCH = 128
DEC_SEQ = 1
PAST_LEN = 16384
PAGE_SIZE = 128

HEAD_DIM = 128
N_HEADS_A = 8
N_HEADS_B = 8
D_A = N_HEADS_A * HEAD_DIM
D_B = N_HEADS_B * HEAD_DIM
D_MIX = D_A + D_B
D_IN = 2 * D_A + 2 * D_B
CHUNK = 128
CONV_W = 4
LRU_C = 8.0
D_FF = 5632
EPS = 1e-6

kernel_name = "hymba_gmlp_rglru_macaron_step"


def rmsnorm(x, g):
    xf = x.astype(jnp.float32)
    r = lax.rsqrt(jnp.mean(xf * xf, axis=-1, keepdims=True) + EPS)
    return (xf * r * g.astype(jnp.float32)).astype(x.dtype)


def swiglu(x, wg, wu, wd):
    return (jax.nn.silu(x @ wg) * (x @ wu)) @ wd


def chunk_mlp(u, v, v_gain, w_s, b_s):
    B, L, _ = v.shape
    vh = rmsnorm(v.reshape(B, L, N_HEADS_A, HEAD_DIM), v_gain.reshape(N_HEADS_A, HEAD_DIM))
    n_chunks = -(-L // CHUNK)
    pad = n_chunks * CHUNK - L
    vp = jnp.pad(vh, ((0, 0), (0, pad), (0, 0), (0, 0))).reshape(B, n_chunks, CHUNK, N_HEADS_A, HEAD_DIM)
    mask = jnp.tril(jnp.ones((CHUNK, CHUNK), dtype=w_s.dtype))
    ws = w_s * mask
    z = jnp.einsum('hts,bcshd->bcthd', ws, vp) + jnp.transpose(b_s)[None, None, :, :, None]
    z = z.reshape(B, n_chunks * CHUNK, N_HEADS_A, HEAD_DIM)[:, :L].reshape(B, L, D_A)
    return u * z, vh.reshape(B, L, D_A)


def rg_lru_branch(xb, gb, conv_buf, h0, conv_w, conv_b, w_r, b_r, w_i, b_i, lam):
    B, L, _ = xb.shape
    xc = jnp.concatenate([conv_buf.astype(xb.dtype), xb], axis=1)
    xconv = conv_b + sum(conv_w[k] * xc[:, k:k + L] for k in range(CONV_W))
    new_buf = xc[:, L:]
    xh = xconv.reshape(B, L, N_HEADS_B, HEAD_DIM)
    r = jax.nn.sigmoid(jnp.einsum('blhi,hij->blhj', xh, w_r).reshape(B, L, D_B) + b_r)
    i = jax.nn.sigmoid(jnp.einsum('blhi,hij->blhj', xh, w_i).reshape(B, L, D_B) + b_i)
    log_a = -LRU_C * jax.nn.softplus(-lam.astype(jnp.float32)) * r.astype(jnp.float32)
    a = jnp.exp(log_a)
    mult = jnp.sqrt(-jnp.expm1(2.0 * log_a))
    bx = mult * (i * xconv).astype(jnp.float32)

    def step(h, ab):
        a_t, b_t = ab
        h = a_t * h + b_t
        return h, h

    h_last, hs = lax.scan(step, h0.astype(jnp.float32), (jnp.moveaxis(a, 1, 0), jnp.moveaxis(bx, 1, 0)))
    hs = jnp.moveaxis(hs, 0, 1).astype(xb.dtype)
    y = hs * jax.nn.gelu(gb)
    return y, new_buf, h_last.astype(xb.dtype)


def trunk(x, conv0, h0, ffn1_norm, ffn1_wg, ffn1_wu, ffn1_wd, mix_norm, w_in, v_norm, w_spatial, b_spatial,
          conv_w, conv_b, w_rgate, b_rgate, w_igate, b_igate, lru_lambda, out_norm, w_out,
          ffn2_norm, ffn2_wg, ffn2_wu, ffn2_wd, final_norm):
    L = x.shape[1]
    cur = L - ((L - 1) // CHUNK) * CHUNK
    convs, hs, vs = [], [], []
    for l in range(DEPTH):
        x = x + 0.5 * swiglu(rmsnorm(x, ffn1_norm[l]), ffn1_wg[l], ffn1_wu[l], ffn1_wd[l])
        hn = rmsnorm(x, mix_norm[l])
        proj = hn @ w_in[l]
        u = proj[..., :D_A]
        v = proj[..., D_A:2 * D_A]
        xb = proj[..., 2 * D_A:2 * D_A + D_B]
        gb = proj[..., 2 * D_A + D_B:]
        a_out, v_rows = chunk_mlp(u, v, v_norm[l], w_spatial[l], b_spatial[l])
        b_out, buf, h_last = rg_lru_branch(xb, gb, conv0[l], h0[l], conv_w[l], conv_b[l], w_rgate[l], b_rgate[l],
                                           w_igate[l], b_igate[l], lru_lambda[l])
        merged = jnp.concatenate([rmsnorm(a_out, out_norm[l, :D_A]), rmsnorm(b_out, out_norm[l, D_A:])], axis=-1)
        x = x + merged @ w_out[l]
        x = x + 0.5 * swiglu(rmsnorm(x, ffn2_norm[l]), ffn2_wg[l], ffn2_wu[l], ffn2_wd[l])
        convs.append(buf)
        hs.append(h_last)
        vs.append(v_rows[:, L - cur:])
    y = rmsnorm(x, final_norm)
    return y, jnp.stack(convs), jnp.stack(hs), jnp.stack(vs)


def setup_inputs(seed: int = 0) -> dict:
    key = jax.random.key(seed)
    ks = jax.random.split(key, 32)
    f32 = jnp.float32

    def nrm(k, shape, scale):
        return jax.random.normal(k, shape, f32) * scale

    def gain(k, shape):
        return 1.0 + 0.01 * jax.random.normal(k, shape, f32)

    a0 = jax.random.uniform(ks[20], (DEPTH, D_B), f32, 0.9, 0.999)
    return {
        "x_prompt": nrm(ks[0], (BATCH, SEQ, D_MODEL), 1.0),
        "x_sample": nrm(ks[1], (DEC_BATCH, DEC_SEQ, D_MODEL), 1.0),
        "state_conv": nrm(ks[2], (DEPTH, DEC_BATCH, CONV_W - 1, D_B), 1.0),
        "state_h": nrm(ks[3], (DEPTH, DEC_BATCH, D_B), 0.5),
        "ffn1_norm": gain(ks[4], (DEPTH, D_MODEL)),
        "ffn1_wg": nrm(ks[5], (DEPTH, D_MODEL, D_FF), D_MODEL ** -0.5),
        "ffn1_wu": nrm(ks[6], (DEPTH, D_MODEL, D_FF), D_MODEL ** -0.5),
        "ffn1_wd": nrm(ks[7], (DEPTH, D_FF, D_MODEL), D_FF ** -0.5),
        "mix_norm": gain(ks[8], (DEPTH, D_MODEL)),
        "w_in": nrm(ks[9], (DEPTH, D_MODEL, D_IN), D_MODEL ** -0.5),
        "v_norm": gain(ks[10], (DEPTH, D_A)),
        "w_spatial": nrm(ks[11], (DEPTH, N_HEADS_A, CHUNK, CHUNK), CHUNK ** -0.5),
        "b_spatial": 1.0 + 0.1 * jax.random.normal(ks[12], (DEPTH, N_HEADS_A, CHUNK), f32),
        "conv_w": nrm(ks[13], (DEPTH, CONV_W, D_B), CONV_W ** -0.5),
        "conv_b": nrm(ks[14], (DEPTH, D_B), 0.01),
        "w_rgate": nrm(ks[15], (DEPTH, N_HEADS_B, HEAD_DIM, HEAD_DIM), HEAD_DIM ** -0.5),
        "b_rgate": nrm(ks[16], (DEPTH, D_B), 0.01),
        "w_igate": nrm(ks[17], (DEPTH, N_HEADS_B, HEAD_DIM, HEAD_DIM), HEAD_DIM ** -0.5),
        "b_igate": nrm(ks[18], (DEPTH, D_B), 0.01),
        "lru_lambda": jnp.log(a0) - jnp.log1p(-a0),
        "out_norm": gain(ks[19], (DEPTH, D_MIX)),
        "w_out": nrm(ks[21], (DEPTH, D_MIX, D_MODEL), D_MIX ** -0.5),
        "ffn2_norm": gain(ks[22], (DEPTH, D_MODEL)),
        "ffn2_wg": nrm(ks[23], (DEPTH, D_MODEL, D_FF), D_MODEL ** -0.5),
        "ffn2_wu": nrm(ks[24], (DEPTH, D_MODEL, D_FF), D_MODEL ** -0.5),
        "ffn2_wd": nrm(ks[25], (DEPTH, D_FF, D_MODEL), D_FF ** -0.5),
        "final_norm": gain(ks[26], (D_MODEL,)),
    }


def reference(x_prompt, x_sample, state_conv, state_h, ffn1_norm, ffn1_wg, ffn1_wu, ffn1_wd, mix_norm, w_in,
              v_norm, w_spatial, b_spatial, conv_w, conv_b, w_rgate, b_rgate, w_igate, b_igate, lru_lambda,
              out_norm, w_out, ffn2_norm, ffn2_wg, ffn2_wu, ffn2_wd, final_norm):
    weights = (ffn1_norm, ffn1_wg, ffn1_wu, ffn1_wd, mix_norm, w_in, v_norm, w_spatial, b_spatial,
               conv_w, conv_b, w_rgate, b_rgate, w_igate, b_igate, lru_lambda, out_norm, w_out,
               ffn2_norm, ffn2_wg, ffn2_wu, ffn2_wd, final_norm)
    B = x_prompt.shape[0]
    conv0_p = jnp.zeros((DEPTH, B, CONV_W - 1, D_B), x_prompt.dtype)
    h0_p = jnp.zeros((DEPTH, B, D_B), x_prompt.dtype)
    y_prompt, conv_p, h_p, v_p = trunk(x_prompt, conv0_p, h0_p, *weights)
    y_sample, conv_s, h_s, v_s = trunk(x_sample, state_conv, state_h, *weights)
    return (y_prompt, y_sample, conv_p, h_p, v_p, conv_s, h_s, v_s)
```

```python
import functools

import jax
import jax.numpy as jnp
from jax import lax
from jax.experimental import pallas as pl
from jax.experimental.pallas import tpu as pltpu

F32 = jnp.float32
BF16 = jnp.bfloat16

HEAD_DIM = 128
CHUNK = 128
CONV_W = 4
LRU_C = 8.0
EPS = 1e-6

V7X_SUBLANES = 8
V7X_VMEM_LIMIT_BYTES = 56 * 1024 * 1024

ROW_TILES = 8
FF_TILE = 512
N_TILE = 512
MIX_ROWS = 512


def _params(*sem):
    return pltpu.CompilerParams(dimension_semantics=sem, vmem_limit_bytes=V7X_VMEM_LIMIT_BYTES)


def _rms(x, g):
    r = lax.rsqrt(jnp.mean(x * x, axis=-1, keepdims=True) + EPS)
    return x * r * g


def _gelu_tanh(x):
    c = 0.7978845608028654
    return x * (0.5 * (1.0 + jnp.tanh(c * (x + 0.044715 * (x * x * x)))))


def _ffn_body(x_ref, g_ref, wg_ref, wu_ref, wd_ref, o_ref, xn_ref):
    @pl.when(pl.program_id(1) == 0)
    def _():
        x = x_ref[...]
        xn_ref[...] = _rms(x, g_ref[...]).astype(BF16)
        o_ref[...] = x

    xn = xn_ref[...]
    gate = jnp.dot(xn, wg_ref[...], preferred_element_type=F32)
    up = jnp.dot(xn, wu_ref[...], preferred_element_type=F32)
    h = (gate * jax.nn.sigmoid(gate) * up * 0.5).astype(BF16)
    o_ref[...] += jnp.dot(h, wd_ref[...], preferred_element_type=F32)


def _ffn(x, gain, wg, wu, wd, layer):
    rows, d = x.shape
    d_ff = wg.shape[-1]
    bm = rows // ROW_TILES
    return pl.pallas_call(
        _ffn_body,
        grid=(ROW_TILES, d_ff // FF_TILE),
        in_specs=[
            pl.BlockSpec((bm, d), lambda i, f: (i, 0), pipeline_mode=pl.Buffered(1)),
            pl.BlockSpec((None, 1, d), lambda i, f: (layer, 0, 0)),
            pl.BlockSpec((None, d, FF_TILE), lambda i, f: (layer, 0, f)),
            pl.BlockSpec((None, d, FF_TILE), lambda i, f: (layer, 0, f)),
            pl.BlockSpec((None, FF_TILE, d), lambda i, f: (layer, f, 0)),
        ],
        out_specs=pl.BlockSpec((bm, d), lambda i, f: (i, 0)),
        out_shape=jax.ShapeDtypeStruct((rows, d), F32),
        scratch_shapes=[pltpu.VMEM((bm, d), BF16)],
        compiler_params=_params("arbitrary", "arbitrary"),
        name="ffn",
    )(x, gain, wg, wu, wd)


def _inproj_body(x_ref, g_ref, w_ref, o_ref, xn_ref):
    @pl.when(pl.program_id(1) == 0)
    def _():
        xn_ref[...] = _rms(x_ref[...], g_ref[...]).astype(BF16)

    o_ref[...] = jnp.dot(xn_ref[...], w_ref[...], preferred_element_type=F32)


def _inproj(x, gain, w, layer):
    rows, d = x.shape
    n = w.shape[-1]
    bm = rows // ROW_TILES
    return pl.pallas_call(
        _inproj_body,
        grid=(ROW_TILES, n // N_TILE),
        in_specs=[
            pl.BlockSpec((bm, d), lambda i, j: (i, 0)),
            pl.BlockSpec((None, 1, d), lambda i, j: (layer, 0, 0)),
            pl.BlockSpec((None, d, N_TILE), lambda i, j: (layer, 0, j)),
        ],
        out_specs=pl.BlockSpec((bm, N_TILE), lambda i, j: (i, j)),
        out_shape=jax.ShapeDtypeStruct((rows, n), F32),
        scratch_shapes=[pltpu.VMEM((bm, d), BF16)],
        compiler_params=_params("arbitrary", "arbitrary"),
        name="inproj",
    )(x, gain, w)


def _outproj_body(m_ref, w_ref, x_ref, o_ref):
    o_ref[...] = x_ref[...] + jnp.dot(m_ref[...], w_ref[...], preferred_element_type=F32)


def _outproj(merged, w, x, layer):
    rows, d = x.shape
    k = merged.shape[-1]
    bm = rows // ROW_TILES
    return pl.pallas_call(
        _outproj_body,
        grid=(ROW_TILES, d // N_TILE),
        in_specs=[
            pl.BlockSpec((bm, k), lambda i, j: (i, 0)),
            pl.BlockSpec((None, k, N_TILE), lambda i, j: (layer, 0, j)),
            pl.BlockSpec((bm, N_TILE), lambda i, j: (i, j)),
        ],
        out_specs=pl.BlockSpec((bm, N_TILE), lambda i, j: (i, j)),
        out_shape=jax.ShapeDtypeStruct((rows, d), F32),
        compiler_params=_params("arbitrary", "arbitrary"),
        name="outproj",
    )(merged, w, x)


def _norm_body(x_ref, g_ref, o_ref):
    o_ref[...] = _rms(x_ref[...], g_ref[...])


def _final_norm(x, gain, rows_per_block, first_block, n_blocks):
    d = x.shape[-1]
    return pl.pallas_call(
        _norm_body,
        grid=(n_blocks,),
        in_specs=[
            pl.BlockSpec((rows_per_block, d), lambda i: (first_block + i, 0)),
            pl.BlockSpec((1, d), lambda i: (0, 0)),
        ],
        out_specs=pl.BlockSpec((rows_per_block, d), lambda i: (i, 0)),
        out_shape=jax.ShapeDtypeStruct((rows_per_block * n_blocks, d), F32),
        compiler_params=_params("arbitrary"),
        name="final_norm",
    )(x, gain)


def _lru_gates(xconv, wri, b_r, b_i, c_lam):
    ri = jnp.dot(xconv.astype(BF16), wri, preferred_element_type=F32)
    r = jax.nn.sigmoid(ri[:, :HEAD_DIM] + b_r)
    i = jax.nn.sigmoid(ri[:, HEAD_DIM:] + b_i)
    log_a = c_lam * r
    a = jnp.exp(log_a)
    mult = jnp.sqrt(-jnp.tanh(log_a) * (a * a + 1.0))
    return a, mult * (i * xconv)


def _mix_prompt_body(p_ref, vg_ref, ws_ref, bs_ref, cw_ref, cb_ref, wri_ref, br_ref, bi_ref, lam_ref, on_ref,
                     mg_ref, conv_ref, h_ref, v_ref,
                     xc_scr, a_scr, b_scr, ao_scr, hc_scr):
    rows = p_ref.shape[0]
    d_a = ws_ref.shape[0] * HEAD_DIM
    d_b = wri_ref.shape[0] * HEAD_DIM
    n_grp = rows // V7X_SUBLANES

    @pl.when(pl.program_id(1) == 0)
    def _():
        xc_scr[0:V7X_SUBLANES, :] = jnp.zeros((V7X_SUBLANES, d_b), F32)
        hc_scr[...] = jnp.zeros_like(hc_scr)

    rr = lax.broadcasted_iota(jnp.int32, (CHUNK, CHUNK), 0)
    cc = lax.broadcasted_iota(jnp.int32, (CHUNK, CHUNK), 1)
    tril = (cc <= rr).astype(F32)
    for h in range(d_a // HEAD_DIM):
        cols = slice(h * HEAD_DIM, (h + 1) * HEAD_DIM)
        vh = _rms(p_ref[:, d_a + h * HEAD_DIM:d_a + (h + 1) * HEAD_DIM], vg_ref[:, cols])
        v_ref[:, cols] = vh[rows - CHUNK:, :]
        vhb = vh.astype(BF16)
        wsm = (ws_ref[h] * tril).astype(BF16)
        bcol = bs_ref[:, h:h + 1]
        for c in range(rows // CHUNK):
            rs = slice(c * CHUNK, (c + 1) * CHUNK)
            z = jnp.dot(wsm, vhb[rs, :], preferred_element_type=F32) + bcol
            ao_scr[rs, cols] = p_ref[rs, cols] * z
    mg_ref[:, :d_a] = _rms(ao_scr[...], on_ref[:, :d_a]).astype(BF16)

    lead = V7X_SUBLANES
    xc_scr[lead:lead + rows, :] = p_ref[:, 2 * d_a:2 * d_a + d_b]
    conv_ref[...] = xc_scr[lead + rows - (CONV_W - 1):lead + rows, :]
    c_lam = -LRU_C * jax.nn.softplus(-lam_ref[...])
    sub = lax.broadcasted_iota(jnp.int32, (n_grp, V7X_SUBLANES, HEAD_DIM), 1)
    for h in range(d_b // HEAD_DIM):
        cols = slice(h * HEAD_DIM, (h + 1) * HEAD_DIM)
        acc = cw_ref[0:1, cols] * xc_scr[lead - 3:lead - 3 + rows, cols]
        for k in range(1, CONV_W):
            acc = acc + cw_ref[k:k + 1, cols] * xc_scr[lead - 3 + k:lead - 3 + k + rows, cols]
        xconv = cb_ref[:, cols] + acc
        a, bx = _lru_gates(xconv, wri_ref[h], br_ref[:, cols], bi_ref[:, cols], c_lam[:, cols])
        a3 = a.reshape(n_grp, V7X_SUBLANES, HEAD_DIM)
        b3 = bx.reshape(n_grp, V7X_SUBLANES, HEAD_DIM)
        for k in (1, 2, 4):
            a_prev = pltpu.roll(a3, k, axis=1)
            b_prev = pltpu.roll(b3, k, axis=1)
            keep = sub >= k
            b3 = jnp.where(keep, b3 + a3 * b_prev, b3)
            a3 = jnp.where(keep, a3 * a_prev, a3)
        a_scr[:, cols] = a3.reshape(rows, HEAD_DIM)
        b_scr[:, cols] = b3.reshape(rows, HEAD_DIM)
    xc_scr[0:lead, :] = xc_scr[rows:rows + lead, :]

    def group_step(g, h_prev):
        r0 = pl.multiple_of(g * V7X_SUBLANES, V7X_SUBLANES)
        hs = b_scr[pl.ds(r0, V7X_SUBLANES), :] + a_scr[pl.ds(r0, V7X_SUBLANES), :] * h_prev
        b_scr[pl.ds(r0, V7X_SUBLANES), :] = hs
        return jnp.broadcast_to(hs[V7X_SUBLANES - 1:, :], (V7X_SUBLANES, d_b))

    h_fin = lax.fori_loop(0, n_grp, group_step, hc_scr[...])
    hc_scr[...] = h_fin
    h_ref[...] = h_fin[0:1, :]

    for h in range(d_b // HEAD_DIM):
        cols = slice(h * HEAD_DIM, (h + 1) * HEAD_DIM)
        gb = p_ref[:, 2 * d_a + d_b + h * HEAD_DIM:2 * d_a + d_b + (h + 1) * HEAD_DIM]
        ao_scr[:, cols] = b_scr[:, cols] * _gelu_tanh(gb)
    mg_ref[:, d_a:] = _rms(ao_scr[...], on_ref[:, d_a:]).astype(BF16)


def _mix_prompt(proj, batch, seq, total_rows, layer, vg, ws, bs_t, cw, cb, wri, br, bi, lam, on):
    d_in = proj.shape[-1]
    n_a, n_b = ws.shape[1], wri.shape[1]
    d_a, d_b = n_a * HEAD_DIM, n_b * HEAD_DIM
    tiles = seq // MIX_ROWS
    vec = lambda n: pl.BlockSpec((None, 1, n), lambda b, t: (layer, 0, 0))
    return pl.pallas_call(
        _mix_prompt_body,
        grid=(batch, tiles),
        in_specs=[
            pl.BlockSpec((MIX_ROWS, d_in), lambda b, t: (b * tiles + t, 0)),
            vec(d_a),
            pl.BlockSpec((None, n_a, CHUNK, CHUNK), lambda b, t: (layer, 0, 0, 0)),
            pl.BlockSpec((None, CHUNK, n_a), lambda b, t: (layer, 0, 0)),
            pl.BlockSpec((None, CONV_W, d_b), lambda b, t: (layer, 0, 0)),
            vec(d_b),
            pl.BlockSpec((None, n_b, HEAD_DIM, 2 * HEAD_DIM), lambda b, t: (layer, 0, 0, 0)),
            vec(d_b), vec(d_b), vec(d_b), vec(d_a + d_b),
        ],
        out_specs=[
            pl.BlockSpec((MIX_ROWS, d_a + d_b), lambda b, t: (b * tiles + t, 0)),
            pl.BlockSpec((None, CONV_W - 1, d_b), lambda b, t: (b, 0, 0)),
            pl.BlockSpec((None, 1, d_b), lambda b, t: (b, 0, 0)),
            pl.BlockSpec((None, CHUNK, d_a), lambda b, t: (b, 0, 0)),
        ],
        out_shape=[
            jax.ShapeDtypeStruct((total_rows, d_a + d_b), BF16),
            jax.ShapeDtypeStruct((batch, CONV_W - 1, d_b), F32),
            jax.ShapeDtypeStruct((batch, 1, d_b), F32),
            jax.ShapeDtypeStruct((batch, CHUNK, d_a), F32),
        ],
        scratch_shapes=[
            pltpu.VMEM((MIX_ROWS + 2 * V7X_SUBLANES, d_b), F32),
            pltpu.VMEM((MIX_ROWS, d_b), F32),
            pltpu.VMEM((MIX_ROWS, d_b), F32),
            pltpu.VMEM((MIX_ROWS, d_a), F32),
            pltpu.VMEM((V7X_SUBLANES, d_b), F32),
        ],
        compiler_params=_params("arbitrary", "arbitrary"),
        name="mix_prompt",
    )(proj, vg, ws, bs_t, cw, cb, wri, br, bi, lam, on)


def _mix_sample_body(p_ref, mg_in_ref, sc_ref, h0_ref, vg_ref, ws0_ref, bs0_ref, cw_ref, cb_ref, wri_ref, br_ref,
                     bi_ref, lam_ref, on_ref,
                     mg_ref, conv_ref, h_ref, v_ref, y_scr):
    del mg_in_ref
    d_a = vg_ref.shape[-1]
    d_b = wri_ref.shape[0] * HEAD_DIM

    for h in range(d_a // HEAD_DIM):
        cols = slice(h * HEAD_DIM, (h + 1) * HEAD_DIM)
        vh = _rms(p_ref[:, d_a + h * HEAD_DIM:d_a + (h + 1) * HEAD_DIM], vg_ref[:, cols])
        v_ref[:, cols] = vh
        y_scr[:, cols] = p_ref[:, cols] * (ws0_ref[:, cols] * vh + bs0_ref[:, cols])
    mg_ref[:, :d_a] = _rms(y_scr[...], on_ref[:, :d_a]).astype(BF16)

    xb = p_ref[:, 2 * d_a:2 * d_a + d_b]
    for k in range(CONV_W - 2):
        conv_ref[k] = sc_ref[k + 1]
    conv_ref[CONV_W - 2] = xb
    c_lam = -LRU_C * jax.nn.softplus(-lam_ref[...])
    for h in range(d_b // HEAD_DIM):
        cols = slice(h * HEAD_DIM, (h + 1) * HEAD_DIM)
        acc = cw_ref[0:1, cols] * sc_ref[0, :, cols]
        for k in range(1, CONV_W - 1):
            acc = acc + cw_ref[k:k + 1, cols] * sc_ref[k, :, cols]
        acc = acc + cw_ref[CONV_W - 1:CONV_W, cols] * xb[:, cols]
        xconv = cb_ref[:, cols] + acc
        a, bx = _lru_gates(xconv, wri_ref[h], br_ref[:, cols], bi_ref[:, cols], c_lam[:, cols])
        hs = a * h0_ref[:, cols] + bx
        h_ref[:, cols] = hs
        gb = p_ref[:, 2 * d_a + d_b + h * HEAD_DIM:2 * d_a + d_b + (h + 1) * HEAD_DIM]
        y_scr[:, cols] = hs * _gelu_tanh(gb)
    mg_ref[:, d_a:] = _rms(y_scr[...], on_ref[:, d_a:]).astype(BF16)


def _mix_sample(proj, merged, n_rows, first_row, layer, sc_t, h0, vg, ws0, bs0, cw, cb, wri, br, bi, lam, on):
    d_in = proj.shape[-1]
    n_b = wri.shape[1]
    d_a = vg.shape[-1]
    d_b = n_b * HEAD_DIM
    blk = first_row // n_rows
    vec = lambda n: pl.BlockSpec((None, 1, n), lambda i: (layer, 0, 0))
    return pl.pallas_call(
        _mix_sample_body,
        grid=(1,),
        in_specs=[
            pl.BlockSpec((n_rows, d_in), lambda i: (blk, 0)),
            pl.BlockSpec(memory_space=pl.ANY),
            pl.BlockSpec((None, CONV_W - 1, n_rows, d_b), lambda i: (layer, 0, 0, 0)),
            pl.BlockSpec((None, n_rows, d_b), lambda i: (layer, 0, 0)),
            vec(d_a), vec(d_a), vec(d_a),
            pl.BlockSpec((None, CONV_W, d_b), lambda i: (layer, 0, 0)),
            vec(d_b),
            pl.BlockSpec((None, n_b, HEAD_DIM, 2 * HEAD_DIM), lambda i: (layer, 0, 0, 0)),
            vec(d_b), vec(d_b), vec(d_b), vec(d_a + d_b),
        ],
        out_specs=[
            pl.BlockSpec((n_rows, d_a + d_b), lambda i: (blk, 0)),
            pl.BlockSpec((CONV_W - 1, n_rows, d_b), lambda i: (0, 0, 0)),
            pl.BlockSpec((n_rows, d_b), lambda i: (0, 0)),
            pl.BlockSpec((n_rows, d_a), lambda i: (0, 0)),
        ],
        out_shape=[
            jax.ShapeDtypeStruct(merged.shape, merged.dtype),
            jax.ShapeDtypeStruct((CONV_W - 1, n_rows, d_b), F32),
            jax.ShapeDtypeStruct((n_rows, d_b), F32),
            jax.ShapeDtypeStruct((n_rows, d_a), F32),
        ],
        scratch_shapes=[pltpu.VMEM((n_rows, d_a), F32)],
        input_output_aliases={1: 0},
        compiler_params=_params("arbitrary"),
        name="mix_sample",
    )(proj, merged, sc_t, h0, vg, ws0, bs0, cw, cb, wri, br, bi, lam, on)


def kernel(x_prompt, x_sample, state_conv, state_h, ffn1_norm, ffn1_wg, ffn1_wu, ffn1_wd, mix_norm, w_in, v_norm,
           w_spatial, b_spatial, conv_w, conv_b, w_rgate, b_rgate, w_igate, b_igate, lru_lambda, out_norm, w_out,
           ffn2_norm, ffn2_wg, ffn2_wu, ffn2_wd, final_norm):
    batch, seq, d_model = x_prompt.shape
    dec_batch, dec_seq, _ = x_sample.shape
    depth = w_in.shape[0]
    assert dec_seq == 1 and seq % MIX_ROWS == 0 and MIX_ROWS % CHUNK == 0
    p_rows = batch * seq
    rows = p_rows + dec_batch
    assert rows % (ROW_TILES * 2 * V7X_SUBLANES) == 0 and p_rows % dec_batch == 0

    x = jnp.concatenate([x_prompt.reshape(p_rows, d_model), x_sample.reshape(dec_batch, d_model)], axis=0)

    row3 = lambda a: a.reshape(a.shape[0], 1, a.shape[-1])
    bf = lambda a: a.astype(BF16)
    g1, gm, g2 = row3(ffn1_norm), row3(mix_norm), row3(ffn2_norm)
    wg1, wu1, wd1 = bf(ffn1_wg), bf(ffn1_wu), bf(ffn1_wd)
    wg2, wu2, wd2 = bf(ffn2_wg), bf(ffn2_wu), bf(ffn2_wd)
    w_in_b, w_out_b = bf(w_in), bf(w_out)
    wri = bf(jnp.concatenate([w_rgate, w_igate], axis=-1))
    vg, cb, br, bi, lam, on = (row3(a) for a in (v_norm, conv_b, b_rgate, b_igate, lru_lambda, out_norm))
    bs_t = jnp.transpose(b_spatial, (0, 2, 1))
    ws0 = row3(jnp.repeat(w_spatial[:, :, 0, 0], HEAD_DIM, axis=-1))
    bs0 = row3(jnp.repeat(b_spatial[:, :, 0], HEAD_DIM, axis=-1))
    sc_t = jnp.transpose(state_conv, (0, 2, 1, 3))

    conv_p, h_p, v_p, conv_s, h_s, v_s = [], [], [], [], [], []
    for l in range(depth):
        x = _ffn(x, g1, wg1, wu1, wd1, l)
        proj = _inproj(x, gm, w_in_b, l)
        merged, cp, hp, vp = _mix_prompt(proj, batch, seq, rows, l, vg, w_spatial, bs_t, conv_w, cb, wri, br, bi,
                                         lam, on)
        merged, cs, hs, vs = _mix_sample(proj, merged, dec_batch, p_rows, l, sc_t, state_h, vg, ws0, bs0, conv_w,
                                         cb, wri, br, bi, lam, on)
        x = _outproj(merged, w_out_b, x, l)
        x = _ffn(x, g2, wg2, wu2, wd2, l)
        conv_p.append(cp)
        h_p.append(hp.reshape(batch, -1))
        v_p.append(vp)
        conv_s.append(jnp.transpose(cs, (1, 0, 2)))
        h_s.append(hs)
        v_s.append(vs.reshape(dec_batch, dec_seq, -1))

    gf = final_norm.reshape(1, d_model)
    y_prompt = _final_norm(x, gf, MIX_ROWS, 0, p_rows // MIX_ROWS).reshape(batch, seq, d_model)
    y_sample = _final_norm(x, gf, dec_batch, p_rows // dec_batch, 1).reshape(dec_batch, dec_seq, d_model)
    return (y_prompt, y_sample, jnp.stack(conv_p), jnp.stack(h_p), jnp.stack(v_p),
            jnp.stack(conv_s), jnp.stack(h_s), jnp.stack(v_s))
```

```python
import functools

import jax
import jax.numpy as jnp
from jax import lax
from jax.experimental import pallas as pl
from jax.experimental.pallas import tpu as pltpu

F32 = jnp.float32
BF16 = jnp.bfloat16

HEAD_DIM = 128
CHUNK = 128
CONV_W = 4
LRU_C = 8.0
EPS = 1e-6

V7X_SUBLANES = 8
V7X_VMEM_LIMIT_BYTES = 56 * 1024 * 1024

ROW_TILES = 8
FF_TILE = 512
FF_TILE_HEAD = 256
N_TILE = 512
MIX_ROWS = 512


def _params(*sem):
    return pltpu.CompilerParams(dimension_semantics=sem, vmem_limit_bytes=V7X_VMEM_LIMIT_BYTES)


def _rms(x, g):
    r = lax.rsqrt(jnp.mean(x * x, axis=-1, keepdims=True) + EPS)
    return x * r * g


def _gelu_tanh(x):
    c = 0.7978845608028654
    return x * (0.5 * (1.0 + jnp.tanh(c * (x + 0.044715 * (x * x * x)))))


def _ffn_step(first, x_ref, g_ref, wg, wu, wd, o_ref, xn_ref):
    @pl.when(first)
    def _():
        x = x_ref[...]
        xn_ref[...] = _rms(x, g_ref[...]).astype(BF16)
        o_ref[...] = x

    xn = xn_ref[...]
    gate = jnp.dot(xn, wg, preferred_element_type=F32)
    up = jnp.dot(xn, wu, preferred_element_type=F32)
    h = (gate * jax.nn.sigmoid(gate) * up * 0.5).astype(BF16)
    o_ref[...] += jnp.dot(h, wd, preferred_element_type=F32)


def _ffn_head_body(x_ref, g_ref, wg_ref, wu_ref, wd_ref, o_ref, wgb_ref, wub_ref, wdb_ref, xn_ref):
    wgb_ref[...] = wg_ref[...].astype(BF16)
    wub_ref[...] = wu_ref[...].astype(BF16)
    wdb_ref[...] = wd_ref[...].astype(BF16)
    _ffn_step(pl.program_id(0) == 0, x_ref, g_ref, wgb_ref[...], wub_ref[...], wdb_ref[...], o_ref, xn_ref)


def _ffn_tail_body(x_ref, g_ref, wg_ref, wu_ref, wd_ref, o_in_ref, o_ref, xn_ref):
    del o_in_ref
    _ffn_step(pl.program_id(1) == 0, x_ref, g_ref, wg_ref[...], wu_ref[...], wd_ref[...], o_ref, xn_ref)


def _ffn(x, gain, wg, wu, wd, layer):
    rows, d = x.shape
    d_ff = wg.shape[-1]
    bm = rows // ROW_TILES
    gain_spec = lambda nd: pl.BlockSpec((None, 1, d), (lambda f: (layer, 0, 0)) if nd == 1
                                        else (lambda i, f: (layer, 0, 0)))
    out, wgb, wub, wdb = pl.pallas_call(
        _ffn_head_body,
        grid=(d_ff // FF_TILE_HEAD,),
        in_specs=[
            pl.BlockSpec((bm, d), lambda f: (0, 0), pipeline_mode=pl.Buffered(1)),
            gain_spec(1),
            pl.BlockSpec((None, d, FF_TILE_HEAD), lambda f: (layer, 0, f)),
            pl.BlockSpec((None, d, FF_TILE_HEAD), lambda f: (layer, 0, f)),
            pl.BlockSpec((None, FF_TILE_HEAD, d), lambda f: (layer, f, 0)),
        ],
        out_specs=[
            pl.BlockSpec((bm, d), lambda f: (0, 0)),
            pl.BlockSpec((d, FF_TILE_HEAD), lambda f: (0, f)),
            pl.BlockSpec((d, FF_TILE_HEAD), lambda f: (0, f)),
            pl.BlockSpec((FF_TILE_HEAD, d), lambda f: (f, 0)),
        ],
        out_shape=[
            jax.ShapeDtypeStruct((rows, d), F32),
            jax.ShapeDtypeStruct((d, d_ff), BF16),
            jax.ShapeDtypeStruct((d, d_ff), BF16),
            jax.ShapeDtypeStruct((d_ff, d), BF16),
        ],
        scratch_shapes=[pltpu.VMEM((bm, d), BF16)],
        compiler_params=_params("arbitrary"),
        name="ffn_head",
    )(x, gain, wg, wu, wd)
    return pl.pallas_call(
        _ffn_tail_body,
        grid=(ROW_TILES - 1, d_ff // FF_TILE),
        in_specs=[
            pl.BlockSpec((bm, d), lambda i, f: (i + 1, 0), pipeline_mode=pl.Buffered(1)),
            gain_spec(2),
            pl.BlockSpec((d, FF_TILE), lambda i, f: (0, f)),
            pl.BlockSpec((d, FF_TILE), lambda i, f: (0, f)),
            pl.BlockSpec((FF_TILE, d), lambda i, f: (f, 0)),
            pl.BlockSpec(memory_space=pl.ANY),
        ],
        out_specs=pl.BlockSpec((bm, d), lambda i, f: (i + 1, 0)),
        out_shape=jax.ShapeDtypeStruct((rows, d), F32),
        scratch_shapes=[pltpu.VMEM((bm, d), BF16)],
        input_output_aliases={5: 0},
        compiler_params=_params("arbitrary", "arbitrary"),
        name="ffn_tail",
    )(x, gain, wgb, wub, wdb, out)


def _inproj_body(x_ref, g_ref, w_ref, o_ref, xn_ref):
    @pl.when(pl.program_id(1) == 0)
    def _():
        xn_ref[...] = _rms(x_ref[...], g_ref[...]).astype(BF16)

    o_ref[...] = jnp.dot(xn_ref[...], w_ref[...], preferred_element_type=F32)


def _inproj(x, gain, w, layer):
    rows, d = x.shape
    n = w.shape[-1]
    bm = rows // ROW_TILES
    return pl.pallas_call(
        _inproj_body,
        grid=(ROW_TILES, n // N_TILE),
        in_specs=[
            pl.BlockSpec((bm, d), lambda i, j: (i, 0)),
            pl.BlockSpec((None, 1, d), lambda i, j: (layer, 0, 0)),
            pl.BlockSpec((None, d, N_TILE), lambda i, j: (layer, 0, j)),
        ],
        out_specs=pl.BlockSpec((bm, N_TILE), lambda i, j: (i, j)),
        out_shape=jax.ShapeDtypeStruct((rows, n), F32),
        scratch_shapes=[pltpu.VMEM((bm, d), BF16)],
        compiler_params=_params("arbitrary", "arbitrary"),
        name="inproj",
    )(x, gain, w)


def _outproj_body(m_ref, w_ref, x_ref, o_ref):
    o_ref[...] = x_ref[...] + jnp.dot(m_ref[...], w_ref[...], preferred_element_type=F32)


def _outproj(merged, w, x, layer):
    rows, d = x.shape
    k = merged.shape[-1]
    bm = rows // ROW_TILES
    return pl.pallas_call(
        _outproj_body,
        grid=(ROW_TILES, d // N_TILE),
        in_specs=[
            pl.BlockSpec((bm, k), lambda i, j: (i, 0)),
            pl.BlockSpec((None, k, N_TILE), lambda i, j: (layer, 0, j)),
            pl.BlockSpec((bm, N_TILE), lambda i, j: (i, j)),
        ],
        out_specs=pl.BlockSpec((bm, N_TILE), lambda i, j: (i, j)),
        out_shape=jax.ShapeDtypeStruct((rows, d), F32),
        compiler_params=_params("arbitrary", "arbitrary"),
        name="outproj",
    )(merged, w, x)


def _norm_body(x_ref, g_ref, o_ref):
    o_ref[...] = _rms(x_ref[...], g_ref[...])


def _final_norm(x, gain, rows_per_block, first_block, n_blocks):
    d = x.shape[-1]
    return pl.pallas_call(
        _norm_body,
        grid=(n_blocks,),
        in_specs=[
            pl.BlockSpec((rows_per_block, d), lambda i: (first_block + i, 0)),
            pl.BlockSpec((1, d), lambda i: (0, 0)),
        ],
        out_specs=pl.BlockSpec((rows_per_block, d), lambda i: (i, 0)),
        out_shape=jax.ShapeDtypeStruct((rows_per_block * n_blocks, d), F32),
        compiler_params=_params("arbitrary"),
        name="final_norm",
    )(x, gain)


def _lru_gates(xconv, wri, b_r, b_i, c_lam):
    ri = jnp.dot(xconv.astype(BF16), wri, preferred_element_type=F32)
    r = jax.nn.sigmoid(ri[:, :HEAD_DIM] + b_r)
    i = jax.nn.sigmoid(ri[:, HEAD_DIM:] + b_i)
    log_a = c_lam * r
    a = jnp.exp(log_a)
    mult = jnp.sqrt(-jnp.tanh(log_a) * (a * a + 1.0))
    return a, mult * (i * xconv)


def _mix_prompt_body(p_ref, vg_ref, ws_ref, bs_ref, cw_ref, cb_ref, wri_ref, br_ref, bi_ref, lam_ref, on_ref,
                     mg_ref, conv_ref, h_ref, v_ref,
                     xc_scr, a_scr, b_scr, ao_scr, hc_scr):
    rows = p_ref.shape[0]
    d_a = ws_ref.shape[0] * HEAD_DIM
    d_b = wri_ref.shape[0] * HEAD_DIM
    n_grp = rows // V7X_SUBLANES

    @pl.when(pl.program_id(1) == 0)
    def _():
        xc_scr[0:V7X_SUBLANES, :] = jnp.zeros((V7X_SUBLANES, d_b), F32)
        hc_scr[...] = jnp.zeros_like(hc_scr)

    rr = lax.broadcasted_iota(jnp.int32, (CHUNK, CHUNK), 0)
    cc = lax.broadcasted_iota(jnp.int32, (CHUNK, CHUNK), 1)
    tril = (cc <= rr).astype(F32)
    for h in range(d_a // HEAD_DIM):
        cols = slice(h * HEAD_DIM, (h + 1) * HEAD_DIM)
        vh = _rms(p_ref[:, d_a + h * HEAD_DIM:d_a + (h + 1) * HEAD_DIM], vg_ref[:, cols])
        v_ref[:, cols] = vh[rows - CHUNK:, :]
        vhb = vh.astype(BF16)
        wsm = (ws_ref[h] * tril).astype(BF16)
        bcol = bs_ref[:, h:h + 1]
        for c in range(rows // CHUNK):
            rs = slice(c * CHUNK, (c + 1) * CHUNK)
            z = jnp.dot(wsm, vhb[rs, :], preferred_element_type=F32) + bcol
            ao_scr[rs, cols] = p_ref[rs, cols] * z
    mg_ref[:, :d_a] = _rms(ao_scr[...], on_ref[:, :d_a]).astype(BF16)

    lead = V7X_SUBLANES
    xc_scr[lead:lead + rows, :] = p_ref[:, 2 * d_a:2 * d_a + d_b]
    conv_ref[...] = xc_scr[lead + rows - (CONV_W - 1):lead + rows, :]
    c_lam = -LRU_C * jax.nn.softplus(-lam_ref[...])
    sub = lax.broadcasted_iota(jnp.int32, (n_grp, V7X_SUBLANES, HEAD_DIM), 1)
    for h in range(d_b // HEAD_DIM):
        cols = slice(h * HEAD_DIM, (h + 1) * HEAD_DIM)
        acc = cw_ref[0:1, cols] * xc_scr[lead - 3:lead - 3 + rows, cols]
        for k in range(1, CONV_W):
            acc = acc + cw_ref[k:k + 1, cols] * xc_scr[lead - 3 + k:lead - 3 + k + rows, cols]
        xconv = cb_ref[:, cols] + acc
        a, bx = _lru_gates(xconv, wri_ref[h], br_ref[:, cols], bi_ref[:, cols], c_lam[:, cols])
        a3 = a.reshape(n_grp, V7X_SUBLANES, HEAD_DIM)
        b3 = bx.reshape(n_grp, V7X_SUBLANES, HEAD_DIM)
        for k in (1, 2, 4):
            a_prev = pltpu.roll(a3, k, axis=1)
            b_prev = pltpu.roll(b3, k, axis=1)
            keep = sub >= k
            b3 = jnp.where(keep, b3 + a3 * b_prev, b3)
            a3 = jnp.where(keep, a3 * a_prev, a3)
        a_scr[:, cols] = a3.reshape(rows, HEAD_DIM)
        b_scr[:, cols] = b3.reshape(rows, HEAD_DIM)
    xc_scr[0:lead, :] = xc_scr[rows:rows + lead, :]

    def group_step(g, h_prev):
        r0 = pl.multiple_of(g * V7X_SUBLANES, V7X_SUBLANES)
        hs = b_scr[pl.ds(r0, V7X_SUBLANES), :] + a_scr[pl.ds(r0, V7X_SUBLANES), :] * h_prev
        b_scr[pl.ds(r0, V7X_SUBLANES), :] = hs
        return jnp.broadcast_to(hs[V7X_SUBLANES - 1:, :], (V7X_SUBLANES, d_b))

    h_fin = lax.fori_loop(0, n_grp, group_step, hc_scr[...])
    hc_scr[...] = h_fin
    h_ref[...] = h_fin[0:1, :]

    for h in range(d_b // HEAD_DIM):
        cols = slice(h * HEAD_DIM, (h + 1) * HEAD_DIM)
        gb = p_ref[:, 2 * d_a + d_b + h * HEAD_DIM:2 * d_a + d_b + (h + 1) * HEAD_DIM]
        ao_scr[:, cols] = b_scr[:, cols] * _gelu_tanh(gb)
    mg_ref[:, d_a:] = _rms(ao_scr[...], on_ref[:, d_a:]).astype(BF16)


def _mix_prompt(proj, batch, seq, total_rows, layer, vg, ws, bs_t, cw, cb, wri, br, bi, lam, on):
    d_in = proj.shape[-1]
    n_a, n_b = ws.shape[1], wri.shape[1]
    d_a, d_b = n_a * HEAD_DIM, n_b * HEAD_DIM
    tiles = seq // MIX_ROWS
    vec = lambda n: pl.BlockSpec((None, 1, n), lambda b, t: (layer, 0, 0))
    return pl.pallas_call(
        _mix_prompt_body,
        grid=(batch, tiles),
        in_specs=[
            pl.BlockSpec((MIX_ROWS, d_in), lambda b, t: (b * tiles + t, 0)),
            vec(d_a),
            pl.BlockSpec((None, n_a, CHUNK, CHUNK), lambda b, t: (layer, 0, 0, 0)),
            pl.BlockSpec((None, CHUNK, n_a), lambda b, t: (layer, 0, 0)),
            pl.BlockSpec((None, CONV_W, d_b), lambda b, t: (layer, 0, 0)),
            vec(d_b),
            pl.BlockSpec((None, n_b, HEAD_DIM, 2 * HEAD_DIM), lambda b, t: (layer, 0, 0, 0)),
            vec(d_b), vec(d_b), vec(d_b), vec(d_a + d_b),
        ],
        out_specs=[
            pl.BlockSpec((MIX_ROWS, d_a + d_b), lambda b, t: (b * tiles + t, 0)),
            pl.BlockSpec((None, CONV_W - 1, d_b), lambda b, t: (b, 0, 0)),
            pl.BlockSpec((None, 1, d_b), lambda b, t: (b, 0, 0)),
            pl.BlockSpec((None, CHUNK, d_a), lambda b, t: (b, 0, 0)),
        ],
        out_shape=[
            jax.ShapeDtypeStruct((total_rows, d_a + d_b), BF16),
            jax.ShapeDtypeStruct((batch, CONV_W - 1, d_b), F32),
            jax.ShapeDtypeStruct((batch, 1, d_b), F32),
            jax.ShapeDtypeStruct((batch, CHUNK, d_a), F32),
        ],
        scratch_shapes=[
            pltpu.VMEM((MIX_ROWS + 2 * V7X_SUBLANES, d_b), F32),
            pltpu.VMEM((MIX_ROWS, d_b), F32),
            pltpu.VMEM((MIX_ROWS, d_b), F32),
            pltpu.VMEM((MIX_ROWS, d_a), F32),
            pltpu.VMEM((V7X_SUBLANES, d_b), F32),
        ],
        compiler_params=_params("arbitrary", "arbitrary"),
        name="mix_prompt",
    )(proj, vg, ws, bs_t, cw, cb, wri, br, bi, lam, on)


def _mix_sample_body(p_ref, mg_in_ref, sc_ref, h0_ref, vg_ref, ws0_ref, bs0_ref, cw_ref, cb_ref, wri_ref, br_ref,
                     bi_ref, lam_ref, on_ref,
                     mg_ref, conv_ref, h_ref, v_ref, y_scr):
    del mg_in_ref
    d_a = vg_ref.shape[-1]
    d_b = wri_ref.shape[0] * HEAD_DIM

    for h in range(d_a // HEAD_DIM):
        cols = slice(h * HEAD_DIM, (h + 1) * HEAD_DIM)
        vh = _rms(p_ref[:, d_a + h * HEAD_DIM:d_a + (h + 1) * HEAD_DIM], vg_ref[:, cols])
        v_ref[:, cols] = vh
        y_scr[:, cols] = p_ref[:, cols] * (ws0_ref[:, cols] * vh + bs0_ref[:, cols])
    mg_ref[:, :d_a] = _rms(y_scr[...], on_ref[:, :d_a]).astype(BF16)

    xb = p_ref[:, 2 * d_a:2 * d_a + d_b]
    for k in range(CONV_W - 2):
        conv_ref[k] = sc_ref[k + 1]
    conv_ref[CONV_W - 2] = xb
    c_lam = -LRU_C * jax.nn.softplus(-lam_ref[...])
    for h in range(d_b // HEAD_DIM):
        cols = slice(h * HEAD_DIM, (h + 1) * HEAD_DIM)
        acc = cw_ref[0:1, cols] * sc_ref[0, :, cols]
        for k in range(1, CONV_W - 1):
            acc = acc + cw_ref[k:k + 1, cols] * sc_ref[k, :, cols]
        acc = acc + cw_ref[CONV_W - 1:CONV_W, cols] * xb[:, cols]
        xconv = cb_ref[:, cols] + acc
        a, bx = _lru_gates(xconv, wri_ref[h], br_ref[:, cols], bi_ref[:, cols], c_lam[:, cols])
        hs = a * h0_ref[:, cols] + bx
        h_ref[:, cols] = hs
        gb = p_ref[:, 2 * d_a + d_b + h * HEAD_DIM:2 * d_a + d_b + (h + 1) * HEAD_DIM]
        y_scr[:, cols] = hs * _gelu_tanh(gb)
    mg_ref[:, d_a:] = _rms(y_scr[...], on_ref[:, d_a:]).astype(BF16)


def _mix_sample(proj, merged, n_rows, first_row, layer, sc_t, h0, vg, ws0, bs0, cw, cb, wri, br, bi, lam, on):
    d_in = proj.shape[-1]
    n_b = wri.shape[1]
    d_a = vg.shape[-1]
    d_b = n_b * HEAD_DIM
    blk = first_row // n_rows
    vec = lambda n: pl.BlockSpec((None, 1, n), lambda i: (layer, 0, 0))
    return pl.pallas_call(
        _mix_sample_body,
        grid=(1,),
        in_specs=[
            pl.BlockSpec((n_rows, d_in), lambda i: (blk, 0)),
            pl.BlockSpec(memory_space=pl.ANY),
            pl.BlockSpec((None, CONV_W - 1, n_rows, d_b), lambda i: (layer, 0, 0, 0)),
            pl.BlockSpec((None, n_rows, d_b), lambda i: (layer, 0, 0)),
            vec(d_a), vec(d_a), vec(d_a),
            pl.BlockSpec((None, CONV_W, d_b), lambda i: (layer, 0, 0)),
            vec(d_b),
            pl.BlockSpec((None, n_b, HEAD_DIM, 2 * HEAD_DIM), lambda i: (layer, 0, 0, 0)),
            vec(d_b), vec(d_b), vec(d_b), vec(d_a + d_b),
        ],
        out_specs=[
            pl.BlockSpec((n_rows, d_a + d_b), lambda i: (blk, 0)),
            pl.BlockSpec((CONV_W - 1, n_rows, d_b), lambda i: (0, 0, 0)),
            pl.BlockSpec((n_rows, d_b), lambda i: (0, 0)),
            pl.BlockSpec((n_rows, d_a), lambda i: (0, 0)),
        ],
        out_shape=[
            jax.ShapeDtypeStruct(merged.shape, merged.dtype),
            jax.ShapeDtypeStruct((CONV_W - 1, n_rows, d_b), F32),
            jax.ShapeDtypeStruct((n_rows, d_b), F32),
            jax.ShapeDtypeStruct((n_rows, d_a), F32),
        ],
        scratch_shapes=[pltpu.VMEM((n_rows, d_a), F32)],
        input_output_aliases={1: 0},
        compiler_params=_params("arbitrary"),
        name="mix_sample",
    )(proj, merged, sc_t, h0, vg, ws0, bs0, cw, cb, wri, br, bi, lam, on)


def kernel(x_prompt, x_sample, state_conv, state_h, ffn1_norm, ffn1_wg, ffn1_wu, ffn1_wd, mix_norm, w_in, v_norm,
           w_spatial, b_spatial, conv_w, conv_b, w_rgate, b_rgate, w_igate, b_igate, lru_lambda, out_norm, w_out,
           ffn2_norm, ffn2_wg, ffn2_wu, ffn2_wd, final_norm):
    batch, seq, d_model = x_prompt.shape
    dec_batch, dec_seq, _ = x_sample.shape
    depth = w_in.shape[0]
    assert dec_seq == 1 and seq % MIX_ROWS == 0 and MIX_ROWS % CHUNK == 0
    p_rows = batch * seq
    rows = p_rows + dec_batch
    assert rows % (ROW_TILES * 2 * V7X_SUBLANES) == 0 and p_rows % dec_batch == 0

    x = jnp.concatenate([x_prompt.reshape(p_rows, d_model), x_sample.reshape(dec_batch, d_model)], axis=0)

    row3 = lambda a: a.reshape(a.shape[0], 1, a.shape[-1])
    bf = lambda a: a.astype(BF16)
    g1, gm, g2 = row3(ffn1_norm), row3(mix_norm), row3(ffn2_norm)
    w_in_b, w_out_b = bf(w_in), bf(w_out)
    wri = bf(jnp.concatenate([w_rgate, w_igate], axis=-1))
    vg, cb, br, bi, lam, on = (row3(a) for a in (v_norm, conv_b, b_rgate, b_igate, lru_lambda, out_norm))
    bs_t = jnp.transpose(b_spatial, (0, 2, 1))
    ws0 = row3(jnp.repeat(w_spatial[:, :, 0, 0], HEAD_DIM, axis=-1))
    bs0 = row3(jnp.repeat(b_spatial[:, :, 0], HEAD_DIM, axis=-1))
    sc_t = jnp.transpose(state_conv, (0, 2, 1, 3))

    conv_p, h_p, v_p, conv_s, h_s, v_s = [], [], [], [], [], []
    for l in range(depth):
        x = _ffn(x, g1, ffn1_wg, ffn1_wu, ffn1_wd, l)
        proj = _inproj(x, gm, w_in_b, l)
        merged, cp, hp, vp = _mix_prompt(proj, batch, seq, rows, l, vg, w_spatial, bs_t, conv_w, cb, wri, br, bi,
                                         lam, on)
        merged, cs, hs, vs = _mix_sample(proj, merged, dec_batch, p_rows, l, sc_t, state_h, vg, ws0, bs0, conv_w,
                                         cb, wri, br, bi, lam, on)
        x = _outproj(merged, w_out_b, x, l)
        x = _ffn(x, g2, ffn2_wg, ffn2_wu, ffn2_wd, l)
        conv_p.append(cp)
        h_p.append(hp.reshape(batch, -1))
        v_p.append(vp)
        conv_s.append(jnp.transpose(cs, (1, 0, 2)))
        h_s.append(hs)
        v_s.append(vs.reshape(dec_batch, dec_seq, -1))

    gf = final_norm.reshape(1, d_model)
    y_prompt = _final_norm(x, gf, MIX_ROWS, 0, p_rows // MIX_ROWS).reshape(batch, seq, d_model)
    y_sample = _final_norm(x, gf, dec_batch, p_rows // dec_batch, 1).reshape(dec_batch, dec_seq, d_model)
    return (y_prompt, y_sample, jnp.stack(conv_p), jnp.stack(h_p), jnp.stack(v_p),
            jnp.stack(conv_s), jnp.stack(h_s), jnp.stack(v_s))
```

```python
import functools

import jax
import jax.numpy as jnp
from jax import lax
from jax.experimental import pallas as pl
from jax.experimental.pallas import tpu as pltpu

F32 = jnp.float32
BF16 = jnp.bfloat16

HEAD_DIM = 128
CHUNK = 128
CONV_W = 4
LRU_C = 8.0
EPS = 1e-6

V7X_SUBLANES = 8
V7X_VMEM_LIMIT_BYTES = 56 * 1024 * 1024

ROW_TILES = 8
FF_TILE = 512
FF_TILE_HEAD = 256
N_TILE = 1024
MIX_ROWS = 512


def _params(*sem):
    return pltpu.CompilerParams(dimension_semantics=sem, vmem_limit_bytes=V7X_VMEM_LIMIT_BYTES)


def _rms(x, g):
    r = lax.rsqrt(jnp.mean(x * x, axis=-1, keepdims=True) + EPS)
    return x * r * g


def _gelu_tanh(x):
    c = 0.7978845608028654
    return x * (0.5 * (1.0 + jnp.tanh(c * (x + 0.044715 * (x * x * x)))))


def _ffn_step(first, x_ref, g_ref, weights, o_ref, xn_ref):
    @pl.when(first)
    def _():
        x = x_ref[...]
        xn_ref[...] = _rms(x, g_ref[...]).astype(BF16)
        o_ref[...] = x

    wg, wu, wd = weights()
    xn = xn_ref[...]
    gate = jnp.dot(xn, wg, preferred_element_type=F32)
    up = jnp.dot(xn, wu, preferred_element_type=F32)
    h = (gate * jax.nn.sigmoid(gate) * up * 0.5).astype(BF16)
    o_ref[...] += jnp.dot(h, wd, preferred_element_type=F32)


def _ffn_head_body(x_ref, g_ref, wg_ref, wu_ref, wd_ref, o_ref, wgb_ref, wub_ref, wdb_ref, xn_ref):
    def weights():
        rounded = []
        for src, dst in ((wg_ref, wgb_ref), (wu_ref, wub_ref), (wd_ref, wdb_ref)):
            w = src[...].astype(BF16)
            dst[...] = w
            rounded.append(w)
        return rounded

    _ffn_step(pl.program_id(0) == 0, x_ref, g_ref, weights, o_ref, xn_ref)


def _ffn_tail_body(x_ref, g_ref, wg_ref, wu_ref, wd_ref, o_in_ref, o_ref, xn_ref):
    del o_in_ref
    _ffn_step(pl.program_id(1) == 0, x_ref, g_ref, lambda: (wg_ref[...], wu_ref[...], wd_ref[...]), o_ref, xn_ref)


def _ffn(x, gain, wg, wu, wd, layer):
    rows, d = x.shape
    d_ff = wg.shape[-1]
    bm = rows // ROW_TILES
    gain_spec = lambda nd: pl.BlockSpec((None, 1, d), (lambda f: (layer, 0, 0)) if nd == 1
                                        else (lambda i, f: (layer, 0, 0)))
    out, wgb, wub, wdb = pl.pallas_call(
        _ffn_head_body,
        grid=(d_ff // FF_TILE_HEAD,),
        in_specs=[
            pl.BlockSpec((bm, d), lambda f: (0, 0), pipeline_mode=pl.Buffered(1)),
            gain_spec(1),
            pl.BlockSpec((None, d, FF_TILE_HEAD), lambda f: (layer, 0, f)),
            pl.BlockSpec((None, d, FF_TILE_HEAD), lambda f: (layer, 0, f)),
            pl.BlockSpec((None, FF_TILE_HEAD, d), lambda f: (layer, f, 0)),
        ],
        out_specs=[
            pl.BlockSpec((bm, d), lambda f: (0, 0)),
            pl.BlockSpec((d, FF_TILE_HEAD), lambda f: (0, f)),
            pl.BlockSpec((d, FF_TILE_HEAD), lambda f: (0, f)),
            pl.BlockSpec((FF_TILE_HEAD, d), lambda f: (f, 0)),
        ],
        out_shape=[
            jax.ShapeDtypeStruct((rows, d), F32),
            jax.ShapeDtypeStruct((d, d_ff), BF16),
            jax.ShapeDtypeStruct((d, d_ff), BF16),
            jax.ShapeDtypeStruct((d_ff, d), BF16),
        ],
        scratch_shapes=[pltpu.VMEM((bm, d), BF16)],
        compiler_params=_params("arbitrary"),
        name="ffn_head",
    )(x, gain, wg, wu, wd)
    return pl.pallas_call(
        _ffn_tail_body,
        grid=(ROW_TILES - 1, d_ff // FF_TILE),
        in_specs=[
            pl.BlockSpec((bm, d), lambda i, f: (i + 1, 0), pipeline_mode=pl.Buffered(1)),
            gain_spec(2),
            pl.BlockSpec((d, FF_TILE), lambda i, f: (0, f)),
            pl.BlockSpec((d, FF_TILE), lambda i, f: (0, f)),
            pl.BlockSpec((FF_TILE, d), lambda i, f: (f, 0)),
            pl.BlockSpec(memory_space=pl.ANY),
        ],
        out_specs=pl.BlockSpec((bm, d), lambda i, f: (i + 1, 0)),
        out_shape=jax.ShapeDtypeStruct((rows, d), F32),
        scratch_shapes=[pltpu.VMEM((bm, d), BF16)],
        input_output_aliases={5: 0},
        compiler_params=_params("arbitrary", "arbitrary"),
        name="ffn_tail",
    )(x, gain, wgb, wub, wdb, out)


def _norm_body(x_ref, g_ref, o_ref):
    o_ref[...] = _rms(x_ref[...], g_ref[...])


def _final_norm(x, gain, rows_per_block, first_block, n_blocks):
    d = x.shape[-1]
    return pl.pallas_call(
        _norm_body,
        grid=(n_blocks,),
        in_specs=[
            pl.BlockSpec((rows_per_block, d), lambda i: (first_block + i, 0)),
            pl.BlockSpec((1, d), lambda i: (0, 0)),
        ],
        out_specs=pl.BlockSpec((rows_per_block, d), lambda i: (i, 0)),
        out_shape=jax.ShapeDtypeStruct((rows_per_block * n_blocks, d), F32),
        compiler_params=_params("arbitrary"),
        name="final_norm",
    )(x, gain)


def _proj_steps(j, n_in, x_ref, gm_ref, win_ref, wout_ref, o_ref, xn_scr, p_scr, mg_scr, mix_fn):
    n_out = x_ref.shape[-1] // N_TILE

    @pl.when(j == 0)
    def _():
        xn_scr[...] = _rms(x_ref[...], gm_ref[...]).astype(BF16)

    @pl.when(j < n_in)
    def _():
        p_scr[j] = jnp.dot(xn_scr[...], win_ref[...], preferred_element_type=F32)

    pl.when(j == n_in)(mix_fn)

    for n in range(n_out):
        @pl.when(j == n_in + 1 + n)
        def _():
            o_ref[...] = x_ref[:, n * N_TILE:(n + 1) * N_TILE] + jnp.dot(mg_scr[...], wout_ref[...],
                                                                         preferred_element_type=F32)


def _pcol(p_scr, col):
    return p_scr[col // N_TILE, :, col % N_TILE:col % N_TILE + HEAD_DIM]


def _lru_gates(xconv, wri, b_r, b_i, c_lam):
    ri = jnp.dot(xconv.astype(BF16), wri, preferred_element_type=F32)
    r = jax.nn.sigmoid(ri[:, :HEAD_DIM] + b_r)
    i = jax.nn.sigmoid(ri[:, HEAD_DIM:] + b_i)
    log_a = c_lam * r
    a = jnp.exp(log_a)
    mult = jnp.sqrt(-jnp.tanh(log_a) * (a * a + 1.0))
    return a, mult * (i * xconv)


def _mix_prompt_body(x_ref, gm_ref, win_ref, wout_ref, vg_ref, ws_ref, bs_ref, cw_ref, cb_ref, wri_ref, br_ref,
                     bi_ref, lam_ref, on_ref,
                     o_ref, conv_ref, h_ref, v_ref,
                     xn_scr, p_scr, mg_scr, xc_scr, a_scr, b_scr, ao_scr, hc_scr):
    rows = x_ref.shape[0]
    d_a = ws_ref.shape[0] * HEAD_DIM
    d_b = wri_ref.shape[0] * HEAD_DIM
    n_grp = rows // V7X_SUBLANES
    lead = V7X_SUBLANES
    first_tile = pl.program_id(1) == 0

    def mix():
        @pl.when(first_tile)
        def _():
            xc_scr[0:lead, :] = jnp.zeros((lead, d_b), F32)
            hc_scr[...] = jnp.zeros_like(hc_scr)

        rr = lax.broadcasted_iota(jnp.int32, (CHUNK, CHUNK), 0)
        cc = lax.broadcasted_iota(jnp.int32, (CHUNK, CHUNK), 1)
        tril = (cc <= rr).astype(F32)
        for h in range(d_a // HEAD_DIM):
            cols = slice(h * HEAD_DIM, (h + 1) * HEAD_DIM)
            vh = _rms(_pcol(p_scr, d_a + h * HEAD_DIM), vg_ref[:, cols])
            v_ref[:, cols] = vh[rows - CHUNK:, :]
            vhb = vh.astype(BF16)
            wsm = (ws_ref[h] * tril).astype(BF16)
            bcol = bs_ref[:, h:h + 1]
            u = _pcol(p_scr, h * HEAD_DIM)
            for c in range(rows // CHUNK):
                rs = slice(c * CHUNK, (c + 1) * CHUNK)
                z = jnp.dot(wsm, vhb[rs, :], preferred_element_type=F32) + bcol
                ao_scr[rs, cols] = u[rs, :] * z
        mg_scr[:, :d_a] = _rms(ao_scr[...], on_ref[:, :d_a]).astype(BF16)

        for h in range(d_b // HEAD_DIM):
            cols = slice(h * HEAD_DIM, (h + 1) * HEAD_DIM)
            xc_scr[lead:lead + rows, cols] = _pcol(p_scr, 2 * d_a + h * HEAD_DIM)
        conv_ref[...] = xc_scr[lead + rows - (CONV_W - 1):lead + rows, :]
        c_lam = -LRU_C * jax.nn.softplus(-lam_ref[...])
        sub = lax.broadcasted_iota(jnp.int32, (n_grp, V7X_SUBLANES, HEAD_DIM), 1)
        for h in range(d_b // HEAD_DIM):
            cols = slice(h * HEAD_DIM, (h + 1) * HEAD_DIM)
            acc = cw_ref[0:1, cols] * xc_scr[lead - 3:lead - 3 + rows, cols]
            for k in range(1, CONV_W):
                acc = acc + cw_ref[k:k + 1, cols] * xc_scr[lead - 3 + k:lead - 3 + k + rows, cols]
            xconv = cb_ref[:, cols] + acc
            a, bx = _lru_gates(xconv, wri_ref[h], br_ref[:, cols], bi_ref[:, cols], c_lam[:, cols])
            a3 = a.reshape(n_grp, V7X_SUBLANES, HEAD_DIM)
            b3 = bx.reshape(n_grp, V7X_SUBLANES, HEAD_DIM)
            for k in (1, 2, 4):
                a_prev = pltpu.roll(a3, k, axis=1)
                b_prev = pltpu.roll(b3, k, axis=1)
                keep = sub >= k
                b3 = jnp.where(keep, b3 + a3 * b_prev, b3)
                a3 = jnp.where(keep, a3 * a_prev, a3)
            a_scr[:, cols] = a3.reshape(rows, HEAD_DIM)
            b_scr[:, cols] = b3.reshape(rows, HEAD_DIM)
        xc_scr[0:lead, :] = xc_scr[rows:rows + lead, :]

        def group_step(g, h_prev):
            r0 = pl.multiple_of(g * V7X_SUBLANES, V7X_SUBLANES)
            hs = b_scr[pl.ds(r0, V7X_SUBLANES), :] + a_scr[pl.ds(r0, V7X_SUBLANES), :] * h_prev
            b_scr[pl.ds(r0, V7X_SUBLANES), :] = hs
            return jnp.broadcast_to(hs[V7X_SUBLANES - 1:, :], (V7X_SUBLANES, d_b))

        h_fin = lax.fori_loop(0, n_grp, group_step, hc_scr[...])
        hc_scr[...] = h_fin
        h_ref[...] = h_fin[0:1, :]

        for h in range(d_b // HEAD_DIM):
            cols = slice(h * HEAD_DIM, (h + 1) * HEAD_DIM)
            ao_scr[:, cols] = b_scr[:, cols] * _gelu_tanh(_pcol(p_scr, 2 * d_a + d_b + h * HEAD_DIM))
        mg_scr[:, d_a:] = _rms(ao_scr[...], on_ref[:, d_a:]).astype(BF16)

    _proj_steps(pl.program_id(2), p_scr.shape[0], x_ref, gm_ref, win_ref, wout_ref, o_ref, xn_scr, p_scr, mg_scr,
                mix)


def _mix_sample_body(x_ref, o_in_ref, gm_ref, win_ref, wout_ref, sc_ref, h0_ref, vg_ref, ws0_ref, bs0_ref, cw_ref,
                     cb_ref, wri_ref, br_ref, bi_ref, lam_ref, on_ref,
                     o_ref, conv_ref, h_ref, v_ref,
                     xn_scr, p_scr, mg_scr, y_scr):
    del o_in_ref
    d_a = vg_ref.shape[-1]
    d_b = wri_ref.shape[0] * HEAD_DIM

    def mix():
        for h in range(d_a // HEAD_DIM):
            cols = slice(h * HEAD_DIM, (h + 1) * HEAD_DIM)
            vh = _rms(_pcol(p_scr, d_a + h * HEAD_DIM), vg_ref[:, cols])
            v_ref[:, cols] = vh
            y_scr[:, cols] = _pcol(p_scr, h * HEAD_DIM) * (ws0_ref[:, cols] * vh + bs0_ref[:, cols])
        mg_scr[:, :d_a] = _rms(y_scr[...], on_ref[:, :d_a]).astype(BF16)

        for k in range(CONV_W - 2):
            conv_ref[k] = sc_ref[k + 1]
        c_lam = -LRU_C * jax.nn.softplus(-lam_ref[...])
        for h in range(d_b // HEAD_DIM):
            cols = slice(h * HEAD_DIM, (h + 1) * HEAD_DIM)
            xb = _pcol(p_scr, 2 * d_a + h * HEAD_DIM)
            conv_ref[CONV_W - 2, :, cols] = xb
            acc = cw_ref[0:1, cols] * sc_ref[0, :, cols]
            for k in range(1, CONV_W - 1):
                acc = acc + cw_ref[k:k + 1, cols] * sc_ref[k, :, cols]
            acc = acc + cw_ref[CONV_W - 1:CONV_W, cols] * xb
            xconv = cb_ref[:, cols] + acc
            a, bx = _lru_gates(xconv, wri_ref[h], br_ref[:, cols], bi_ref[:, cols], c_lam[:, cols])
            hs = a * h0_ref[:, cols] + bx
            h_ref[:, cols] = hs
            y_scr[:, cols] = hs * _gelu_tanh(_pcol(p_scr, 2 * d_a + d_b + h * HEAD_DIM))
        mg_scr[:, d_a:] = _rms(y_scr[...], on_ref[:, d_a:]).astype(BF16)

    _proj_steps(pl.program_id(0), p_scr.shape[0], x_ref, gm_ref, win_ref, wout_ref, o_ref, xn_scr, p_scr, mg_scr,
                mix)


def _mixer_layer(x, batch, seq, layer, gm, w_in, w_out, sc_t, h0, vg, ws, bs_t, ws0, bs0, cw, cb, wri, br, bi, lam,
                 on):
    total_rows, d = x.shape
    d_in = w_in.shape[-1]
    n_a, n_b = ws.shape[1], wri.shape[1]
    d_a, d_b = n_a * HEAD_DIM, n_b * HEAD_DIM
    n_in, n_out = d_in // N_TILE, d // N_TILE
    steps = n_in + 1 + n_out
    tiles = seq // MIX_ROWS
    s_rows = total_rows - batch * seq
    s_blk = (batch * seq) // s_rows

    ix = lambda fn: (lambda *g: fn(g[-1]))
    ocol = lambda j: jnp.clip(j - n_in - 1, 0, n_out - 1)
    vec = lambda n: pl.BlockSpec((None, 1, n), ix(lambda j: (layer, 0, 0)))
    full = lambda *shape: pl.BlockSpec((None,) + shape, ix(lambda j: (layer,) + (0,) * len(shape)))
    win = pl.BlockSpec((None, d, N_TILE), ix(lambda j: (layer, 0, jnp.minimum(j, n_in - 1))))
    wout = pl.BlockSpec((None, d_a + d_b, N_TILE), ix(lambda j: (layer, 0, ocol(j))))

    x_new, conv_p, h_p, v_p = pl.pallas_call(
        _mix_prompt_body,
        grid=(batch, tiles, steps),
        in_specs=[
            pl.BlockSpec((MIX_ROWS, d), lambda b, t, j: (b * tiles + t, 0)),
            vec(d), win, wout, vec(d_a),
            full(n_a, CHUNK, CHUNK), full(CHUNK, n_a), full(CONV_W, d_b), vec(d_b),
            full(n_b, HEAD_DIM, 2 * HEAD_DIM), vec(d_b), vec(d_b), vec(d_b), vec(d_a + d_b),
        ],
        out_specs=[
            pl.BlockSpec((MIX_ROWS, N_TILE), lambda b, t, j: (b * tiles + t, ocol(j))),
            pl.BlockSpec((None, CONV_W - 1, d_b), lambda b, t, j: (b, 0, 0)),
            pl.BlockSpec((None, 1, d_b), lambda b, t, j: (b, 0, 0)),
            pl.BlockSpec((None, CHUNK, d_a), lambda b, t, j: (b, 0, 0)),
        ],
        out_shape=[
            jax.ShapeDtypeStruct((total_rows, d), F32),
            jax.ShapeDtypeStruct((batch, CONV_W - 1, d_b), F32),
            jax.ShapeDtypeStruct((batch, 1, d_b), F32),
            jax.ShapeDtypeStruct((batch, CHUNK, d_a), F32),
        ],
        scratch_shapes=[
            pltpu.VMEM((MIX_ROWS, d), BF16),
            pltpu.VMEM((n_in, MIX_ROWS, N_TILE), F32),
            pltpu.VMEM((MIX_ROWS, d_a + d_b), BF16),
            pltpu.VMEM((MIX_ROWS + 2 * V7X_SUBLANES, d_b), F32),
            pltpu.VMEM((MIX_ROWS, d_b), F32),
            pltpu.VMEM((MIX_ROWS, d_b), F32),
            pltpu.VMEM((MIX_ROWS, d_a), F32),
            pltpu.VMEM((V7X_SUBLANES, d_b), F32),
        ],
        compiler_params=_params("arbitrary", "arbitrary", "arbitrary"),
        name="mix_prompt",
    )(x, gm, w_in, w_out, vg, ws, bs_t, cw, cb, wri, br, bi, lam, on)

    x_new, conv_s, h_s, v_s = pl.pallas_call(
        _mix_sample_body,
        grid=(steps,),
        in_specs=[
            pl.BlockSpec((s_rows, d), lambda j: (s_blk, 0)),
            pl.BlockSpec(memory_space=pl.ANY),
            vec(d), win, wout,
            full(CONV_W - 1, s_rows, d_b), full(s_rows, d_b),
            vec(d_a), vec(d_a), vec(d_a), full(CONV_W, d_b), vec(d_b),
            full(n_b, HEAD_DIM, 2 * HEAD_DIM), vec(d_b), vec(d_b), vec(d_b), vec(d_a + d_b),
        ],
        out_specs=[
            pl.BlockSpec((s_rows, N_TILE), lambda j: (s_blk, ocol(j))),
            pl.BlockSpec((CONV_W - 1, s_rows, d_b), lambda j: (0, 0, 0)),
            pl.BlockSpec((s_rows, d_b), lambda j: (0, 0)),
            pl.BlockSpec((s_rows, d_a), lambda j: (0, 0)),
        ],
        out_shape=[
            jax.ShapeDtypeStruct((total_rows, d), F32),
            jax.ShapeDtypeStruct((CONV_W - 1, s_rows, d_b), F32),
            jax.ShapeDtypeStruct((s_rows, d_b), F32),
            jax.ShapeDtypeStruct((s_rows, d_a), F32),
        ],
        scratch_shapes=[
            pltpu.VMEM((s_rows, d), BF16),
            pltpu.VMEM((n_in, s_rows, N_TILE), F32),
            pltpu.VMEM((s_rows, d_a + d_b), BF16),
            pltpu.VMEM((s_rows, d_a), F32),
        ],
        input_output_aliases={1: 0},
        compiler_params=_params("arbitrary"),
        name="mix_sample",
    )(x, x_new, gm, w_in, w_out, sc_t, h0, vg, ws0, bs0, cw, cb, wri, br, bi, lam, on)
    return x_new, conv_p, h_p, v_p, conv_s, h_s, v_s


def kernel(x_prompt, x_sample, state_conv, state_h, ffn1_norm, ffn1_wg, ffn1_wu, ffn1_wd, mix_norm, w_in, v_norm,
           w_spatial, b_spatial, conv_w, conv_b, w_rgate, b_rgate, w_igate, b_igate, lru_lambda, out_norm, w_out,
           ffn2_norm, ffn2_wg, ffn2_wu, ffn2_wd, final_norm):
    batch, seq, d_model = x_prompt.shape
    dec_batch, dec_seq, _ = x_sample.shape
    depth = w_in.shape[0]
    assert dec_seq == 1 and seq % MIX_ROWS == 0 and MIX_ROWS % CHUNK == 0
    p_rows = batch * seq
    rows = p_rows + dec_batch
    assert rows % (ROW_TILES * 2 * V7X_SUBLANES) == 0 and p_rows % dec_batch == 0

    x = jnp.concatenate([x_prompt.reshape(p_rows, d_model), x_sample.reshape(dec_batch, d_model)], axis=0)

    row3 = lambda a: a.reshape(a.shape[0], 1, a.shape[-1])
    bf = lambda a: a.astype(BF16)
    g1, gm, g2 = row3(ffn1_norm), row3(mix_norm), row3(ffn2_norm)
    w_in_b, w_out_b = bf(w_in), bf(w_out)
    wri = bf(jnp.concatenate([w_rgate, w_igate], axis=-1))
    vg, cb, br, bi, lam, on = (row3(a) for a in (v_norm, conv_b, b_rgate, b_igate, lru_lambda, out_norm))
    bs_t = jnp.transpose(b_spatial, (0, 2, 1))
    ws0 = row3(jnp.repeat(w_spatial[:, :, 0, 0], HEAD_DIM, axis=-1))
    bs0 = row3(jnp.repeat(b_spatial[:, :, 0], HEAD_DIM, axis=-1))
    sc_t = jnp.transpose(state_conv, (0, 2, 1, 3))

    conv_p, h_p, v_p, conv_s, h_s, v_s = [], [], [], [], [], []
    for l in range(depth):
        x = _ffn(x, g1, ffn1_wg, ffn1_wu, ffn1_wd, l)
        x, cp, hp, vp, cs, hs, vs = _mixer_layer(x, batch, seq, l, gm, w_in_b, w_out_b, sc_t, state_h, vg,
                                                 w_spatial, bs_t, ws0, bs0, conv_w, cb, wri, br, bi, lam, on)
        x = _ffn(x, g2, ffn2_wg, ffn2_wu, ffn2_wd, l)
        conv_p.append(cp)
        h_p.append(hp.reshape(batch, -1))
        v_p.append(vp)
        conv_s.append(jnp.transpose(cs, (1, 0, 2)))
        h_s.append(hs)
        v_s.append(vs.reshape(dec_batch, dec_seq, -1))

    gf = final_norm.reshape(1, d_model)
    y_prompt = _final_norm(x, gf, MIX_ROWS, 0, p_rows // MIX_ROWS).reshape(batch, seq, d_model)
    y_sample = _final_norm(x, gf, dec_batch, p_rows // dec_batch, 1).reshape(dec_batch, dec_seq, d_model)
    return (y_prompt, y_sample, jnp.stack(conv_p), jnp.stack(h_p), jnp.stack(v_p),
            jnp.stack(conv_s), jnp.stack(h_s), jnp.stack(v_s))
```

```python
import functools

import jax
import jax.numpy as jnp
from jax import lax
from jax.experimental import pallas as pl
from jax.experimental.pallas import tpu as pltpu

F32 = jnp.float32
BF16 = jnp.bfloat16

HEAD_DIM = 128
CHUNK = 128
CONV_W = 4
LRU_C = 8.0
EPS = 1e-6

V7X_SUBLANES = 8
V7X_VMEM_LIMIT_BYTES = 56 * 1024 * 1024

ROW_TILES = 8
FF_TILE = 512
FF_TILE_HEAD = 256
N_TILE = 1024
MIX_ROWS = 256
NORM_ROWS = 1024
PROJ_CHUNK = 256
MLP_HEAD_CHUNKS = (1, 1, 1, 1, 1, 1, 1, 1)
LRU_HEAD_CHUNKS = (2, 2, 2, 2, 2, 2, 2, 2)


def _params(*sem):
    return pltpu.CompilerParams(dimension_semantics=sem, vmem_limit_bytes=V7X_VMEM_LIMIT_BYTES)


def _rms(x, g):
    r = lax.rsqrt(jnp.mean(x * x, axis=-1, keepdims=True) + EPS)
    return x * r * g


def _gelu_tanh(x):
    c = 0.7978845608028654
    return x * (0.5 * (1.0 + jnp.tanh(c * (x + 0.044715 * (x * x * x)))))


def _ffn_step(first, x_ref, g_ref, weights, o_ref, xn_ref):
    @pl.when(first)
    def _():
        x = x_ref[...]
        xn_ref[...] = _rms(x, g_ref[...]).astype(BF16)
        o_ref[...] = x

    wg, wu, wd = weights()
    xn = xn_ref[...]
    gate = jnp.dot(xn, wg, preferred_element_type=F32)
    up = jnp.dot(xn, wu, preferred_element_type=F32)
    h = (gate * jax.nn.sigmoid(gate) * up * 0.5).astype(BF16)
    o_ref[...] += jnp.dot(h, wd, preferred_element_type=F32)


def _ffn_head_body(x_ref, g_ref, wg_ref, wu_ref, wd_ref, o_ref, wgb_ref, wub_ref, wdb_ref, xn_ref):
    def weights():
        rounded = []
        for src, dst in ((wg_ref, wgb_ref), (wu_ref, wub_ref), (wd_ref, wdb_ref)):
            w = src[...].astype(BF16)
            dst[...] = w
            rounded.append(w)
        return rounded

    _ffn_step(pl.program_id(0) == 0, x_ref, g_ref, weights, o_ref, xn_ref)


def _ffn_tail_body(x_ref, g_ref, wg_ref, wu_ref, wd_ref, o_in_ref, o_ref, xn_ref):
    del o_in_ref
    _ffn_step(pl.program_id(1) == 0, x_ref, g_ref, lambda: (wg_ref[...], wu_ref[...], wd_ref[...]), o_ref, xn_ref)


def _ffn(x, gain, wg, wu, wd, layer):
    rows, d = x.shape
    d_ff = wg.shape[-1]
    bm = rows // ROW_TILES
    gain_spec = lambda nd: pl.BlockSpec((None, 1, d), (lambda f: (layer, 0, 0)) if nd == 1
                                        else (lambda i, f: (layer, 0, 0)))
    out, wgb, wub, wdb = pl.pallas_call(
        _ffn_head_body,
        grid=(d_ff // FF_TILE_HEAD,),
        in_specs=[
            pl.BlockSpec((bm, d), lambda f: (0, 0), pipeline_mode=pl.Buffered(1)),
            gain_spec(1),
            pl.BlockSpec((None, d, FF_TILE_HEAD), lambda f: (layer, 0, f)),
            pl.BlockSpec((None, d, FF_TILE_HEAD), lambda f: (layer, 0, f)),
            pl.BlockSpec((None, FF_TILE_HEAD, d), lambda f: (layer, f, 0)),
        ],
        out_specs=[
            pl.BlockSpec((bm, d), lambda f: (0, 0)),
            pl.BlockSpec((d, FF_TILE_HEAD), lambda f: (0, f)),
            pl.BlockSpec((d, FF_TILE_HEAD), lambda f: (0, f)),
            pl.BlockSpec((FF_TILE_HEAD, d), lambda f: (f, 0)),
        ],
        out_shape=[
            jax.ShapeDtypeStruct((rows, d), F32),
            jax.ShapeDtypeStruct((d, d_ff), BF16),
            jax.ShapeDtypeStruct((d, d_ff), BF16),
            jax.ShapeDtypeStruct((d_ff, d), BF16),
        ],
        scratch_shapes=[pltpu.VMEM((bm, d), BF16)],
        compiler_params=_params("arbitrary"),
        name="ffn_head",
    )(x, gain, wg, wu, wd)
    return pl.pallas_call(
        _ffn_tail_body,
        grid=(ROW_TILES - 1, d_ff // FF_TILE),
        in_specs=[
            pl.BlockSpec((bm, d), lambda i, f: (i + 1, 0), pipeline_mode=pl.Buffered(1)),
            gain_spec(2),
            pl.BlockSpec((d, FF_TILE), lambda i, f: (0, f)),
            pl.BlockSpec((d, FF_TILE), lambda i, f: (0, f)),
            pl.BlockSpec((FF_TILE, d), lambda i, f: (f, 0)),
            pl.BlockSpec(memory_space=pl.ANY),
        ],
        out_specs=pl.BlockSpec((bm, d), lambda i, f: (i + 1, 0)),
        out_shape=jax.ShapeDtypeStruct((rows, d), F32),
        scratch_shapes=[pltpu.VMEM((bm, d), BF16)],
        input_output_aliases={5: 0},
        compiler_params=_params("arbitrary", "arbitrary"),
        name="ffn_tail",
    )(x, gain, wgb, wub, wdb, out)


def _norm_body(x_ref, g_ref, o_ref):
    o_ref[...] = _rms(x_ref[...], g_ref[...])


def _final_norm(x, gain, rows_per_block, first_block, n_blocks):
    d = x.shape[-1]
    return pl.pallas_call(
        _norm_body,
        grid=(n_blocks,),
        in_specs=[
            pl.BlockSpec((rows_per_block, d), lambda i: (first_block + i, 0)),
            pl.BlockSpec((1, d), lambda i: (0, 0)),
        ],
        out_specs=pl.BlockSpec((rows_per_block, d), lambda i: (i, 0)),
        out_shape=jax.ShapeDtypeStruct((rows_per_block * n_blocks, d), F32),
        compiler_params=_params("arbitrary"),
        name="final_norm",
    )(x, gain)


def _proj_steps(j, n_in, x_ref, gm_ref, win_ref, wout_ref, o_ref, xn_scr, p_scr, mg_scr, mix_fn):
    n_out = x_ref.shape[-1] // N_TILE

    @pl.when(j == 0)
    def _():
        xn_scr[...] = _rms(x_ref[...], gm_ref[...]).astype(BF16)

    @pl.when(j < n_in)
    def _():
        p_scr[j] = jnp.dot(xn_scr[...], win_ref[...], preferred_element_type=F32)

    pl.when(j == n_in)(mix_fn)

    for n in range(n_out):
        @pl.when(j == n_in + 1 + n)
        def _():
            o_ref[...] = x_ref[:, n * N_TILE:(n + 1) * N_TILE] + jnp.dot(mg_scr[...], wout_ref[...],
                                                                         preferred_element_type=F32)


def _pcol(p_scr, col):
    return p_scr[col // N_TILE, :, col % N_TILE:col % N_TILE + HEAD_DIM]


def _lru_gates(xconv, wri, b_r, b_i, c_lam):
    ri = jnp.dot(xconv.astype(BF16), wri, preferred_element_type=F32)
    r = jax.nn.sigmoid(ri[:, :HEAD_DIM] + b_r)
    i = jax.nn.sigmoid(ri[:, HEAD_DIM:] + b_i)
    log_a = c_lam * r
    a = jnp.exp(log_a)
    mult = jnp.sqrt(-jnp.tanh(log_a) * (a * a + 1.0))
    return a, mult * (i * xconv)


def _mix_prompt_body(tiles, x_ref, gm_ref, win_ref, wout_ref, vg_ref, ws_ref, bs_ref, cw_ref, cb_ref, wri_ref,
                     br_ref, bi_ref, lam_ref, on_ref,
                     o_ref, conv_ref, h_ref, v_ref,
                     p_scr, xs_scr, mg_scr, xc_scr, a_scr, b_scr, ao_scr, hc_scr):
    rows = x_ref.shape[0]
    d_a = ws_ref.shape[0] * HEAD_DIM
    d_b = wri_ref.shape[0] * HEAD_DIM
    n_grp = rows // V7X_SUBLANES
    lead = V7X_SUBLANES
    s = pl.program_id(0)
    p_new, p_mix = s % 2, (s + 1) % 2
    x_new, x_out = s % 3, (s + 1) % 3
    m_mix, m_out = (s + 1) % 2, s % 2
    first_tile = (jnp.maximum(s - 1, 0) % tiles) == 0

    @pl.when(s == 0)
    def _():
        p_scr[1] = jnp.zeros(p_scr.shape[1:], F32)
        xs_scr[1] = jnp.zeros(xs_scr.shape[1:], F32)
        xs_scr[2] = jnp.zeros(xs_scr.shape[1:], F32)
        mg_scr[0] = jnp.zeros(mg_scr.shape[1:], BF16)
        xc_scr[0:lead, :] = jnp.zeros((lead, d_b), F32)
        hc_scr[...] = jnp.zeros_like(hc_scr)

    x = x_ref[...]
    xs_scr[x_new] = x
    xn = _rms(x, gm_ref[...]).astype(BF16)
    mg_out = mg_scr[m_out]
    matmuls = []
    for k in range(wout_ref.shape[-1] // PROJ_CHUNK):
        def out_chunk(cs=pl.ds(k * PROJ_CHUNK, PROJ_CHUNK)):
            o_ref[:, cs] = xs_scr[x_out, :, cs] + jnp.dot(mg_out, wout_ref[:, cs], preferred_element_type=F32)
        matmuls.append(out_chunk)
    for k in range(win_ref.shape[-1] // PROJ_CHUNK):
        def in_chunk(cs=pl.ds(k * PROJ_CHUNK, PROJ_CHUNK)):
            p_scr[p_new, :, cs] = jnp.dot(xn, win_ref[:, cs], preferred_element_type=F32)
        matmuls.append(in_chunk)
    matmuls = iter(matmuls)

    def project(n_chunks):
        for _ in range(n_chunks):
            next(matmuls)()

    pcol = lambda col: p_scr[p_mix, :, col:col + HEAD_DIM]

    rr = lax.broadcasted_iota(jnp.int32, (CHUNK, CHUNK), 0)
    cc = lax.broadcasted_iota(jnp.int32, (CHUNK, CHUNK), 1)
    tril = (cc <= rr).astype(F32)
    for h in range(d_a // HEAD_DIM):
        project(MLP_HEAD_CHUNKS[h])
        cols = slice(h * HEAD_DIM, (h + 1) * HEAD_DIM)
        vh = _rms(pcol(d_a + h * HEAD_DIM), vg_ref[:, cols])
        v_ref[:, cols] = vh[rows - CHUNK:, :]
        vhb = vh.astype(BF16)
        wsm = (ws_ref[h] * tril).astype(BF16)
        bcol = bs_ref[:, h:h + 1]
        u = pcol(h * HEAD_DIM)
        for c in range(rows // CHUNK):
            rs = slice(c * CHUNK, (c + 1) * CHUNK)
            z = jnp.dot(wsm, vhb[rs, :], preferred_element_type=F32) + bcol
            ao_scr[rs, cols] = u[rs, :] * z
    mg_scr[m_mix, :, :d_a] = _rms(ao_scr[...], on_ref[:, :d_a]).astype(BF16)

    xc_scr[0:lead, :] = jnp.where(first_tile, 0.0, xc_scr[0:lead, :])
    for h in range(d_b // HEAD_DIM):
        cols = slice(h * HEAD_DIM, (h + 1) * HEAD_DIM)
        xc_scr[lead:lead + rows, cols] = pcol(2 * d_a + h * HEAD_DIM)
    conv_ref[...] = xc_scr[lead + rows - (CONV_W - 1):lead + rows, :]
    c_lam = -LRU_C * jax.nn.softplus(-lam_ref[...])
    sub = lax.broadcasted_iota(jnp.int32, (n_grp, V7X_SUBLANES, HEAD_DIM), 1)
    for h in range(d_b // HEAD_DIM):
        project(LRU_HEAD_CHUNKS[h])
        cols = slice(h * HEAD_DIM, (h + 1) * HEAD_DIM)
        acc = cw_ref[0:1, cols] * xc_scr[lead - 3:lead - 3 + rows, cols]
        for k in range(1, CONV_W):
            acc = acc + cw_ref[k:k + 1, cols] * xc_scr[lead - 3 + k:lead - 3 + k + rows, cols]
        xconv = cb_ref[:, cols] + acc
        a, bx = _lru_gates(xconv, wri_ref[h], br_ref[:, cols], bi_ref[:, cols], c_lam[:, cols])
        a3 = a.reshape(n_grp, V7X_SUBLANES, HEAD_DIM)
        b3 = bx.reshape(n_grp, V7X_SUBLANES, HEAD_DIM)
        for k in (1, 2, 4):
            a_prev = pltpu.roll(a3, k, axis=1)
            b_prev = pltpu.roll(b3, k, axis=1)
            keep = sub >= k
            b3 = jnp.where(keep, b3 + a3 * b_prev, b3)
            a3 = jnp.where(keep, a3 * a_prev, a3)
        a_scr[:, cols] = a3.reshape(rows, HEAD_DIM)
        b_scr[:, cols] = b3.reshape(rows, HEAD_DIM)
    xc_scr[0:lead, :] = xc_scr[rows:rows + lead, :]

    h_prev = jnp.where(first_tile, 0.0, hc_scr[...])
    for g in range(n_grp):
        gs = slice(g * V7X_SUBLANES, (g + 1) * V7X_SUBLANES)
        hs = b_scr[gs, :] + a_scr[gs, :] * h_prev
        b_scr[gs, :] = hs
        h_prev = jnp.broadcast_to(hs[V7X_SUBLANES - 1:, :], (V7X_SUBLANES, d_b))
    hc_scr[...] = h_prev
    h_ref[...] = h_prev[0:1, :]

    for h in range(d_b // HEAD_DIM):
        cols = slice(h * HEAD_DIM, (h + 1) * HEAD_DIM)
        ao_scr[:, cols] = b_scr[:, cols] * _gelu_tanh(pcol(2 * d_a + d_b + h * HEAD_DIM))
    mg_scr[m_mix, :, d_a:] = _rms(ao_scr[...], on_ref[:, d_a:]).astype(BF16)
    assert next(matmuls, None) is None


def _mix_sample_body(x_ref, o_in_ref, gm_ref, win_ref, wout_ref, sc_ref, h0_ref, vg_ref, ws0_ref, bs0_ref, cw_ref,
                     cb_ref, wri_ref, br_ref, bi_ref, lam_ref, on_ref,
                     o_ref, conv_ref, h_ref, v_ref,
                     xn_scr, p_scr, mg_scr, y_scr):
    del o_in_ref
    d_a = vg_ref.shape[-1]
    d_b = wri_ref.shape[0] * HEAD_DIM

    def mix():
        for h in range(d_a // HEAD_DIM):
            cols = slice(h * HEAD_DIM, (h + 1) * HEAD_DIM)
            vh = _rms(_pcol(p_scr, d_a + h * HEAD_DIM), vg_ref[:, cols])
            v_ref[:, cols] = vh
            y_scr[:, cols] = _pcol(p_scr, h * HEAD_DIM) * (ws0_ref[:, cols] * vh + bs0_ref[:, cols])
        mg_scr[:, :d_a] = _rms(y_scr[...], on_ref[:, :d_a]).astype(BF16)

        for k in range(CONV_W - 2):
            conv_ref[k] = sc_ref[k + 1]
        c_lam = -LRU_C * jax.nn.softplus(-lam_ref[...])
        for h in range(d_b // HEAD_DIM):
            cols = slice(h * HEAD_DIM, (h + 1) * HEAD_DIM)
            xb = _pcol(p_scr, 2 * d_a + h * HEAD_DIM)
            conv_ref[CONV_W - 2, :, cols] = xb
            acc = cw_ref[0:1, cols] * sc_ref[0, :, cols]
            for k in range(1, CONV_W - 1):
                acc = acc + cw_ref[k:k + 1, cols] * sc_ref[k, :, cols]
            acc = acc + cw_ref[CONV_W - 1:CONV_W, cols] * xb
            xconv = cb_ref[:, cols] + acc
            a, bx = _lru_gates(xconv, wri_ref[h], br_ref[:, cols], bi_ref[:, cols], c_lam[:, cols])
            hs = a * h0_ref[:, cols] + bx
            h_ref[:, cols] = hs
            y_scr[:, cols] = hs * _gelu_tanh(_pcol(p_scr, 2 * d_a + d_b + h * HEAD_DIM))
        mg_scr[:, d_a:] = _rms(y_scr[...], on_ref[:, d_a:]).astype(BF16)

    _proj_steps(pl.program_id(0), p_scr.shape[0], x_ref, gm_ref, win_ref, wout_ref, o_ref, xn_scr, p_scr, mg_scr,
                mix)


def _mixer_layer(x, batch, seq, layer, gm, w_in, w_out, sc_t, h0, vg, ws, bs_t, ws0, bs0, cw, cb, wri, br, bi, lam,
                 on):
    total_rows, d = x.shape
    d_in = w_in.shape[-1]
    n_a, n_b = ws.shape[1], wri.shape[1]
    d_a, d_b = n_a * HEAD_DIM, n_b * HEAD_DIM
    n_in, n_out = d_in // N_TILE, d // N_TILE
    steps = n_in + 1 + n_out
    tiles = seq // MIX_ROWS
    s_rows = total_rows - batch * seq
    s_blk = (batch * seq) // s_rows

    ix = lambda fn: (lambda *g: fn(g[-1]))
    ocol = lambda j: jnp.clip(j - n_in - 1, 0, n_out - 1)
    vec = lambda n: pl.BlockSpec((None, 1, n), ix(lambda j: (layer, 0, 0)))
    full = lambda *shape: pl.BlockSpec((None,) + shape, ix(lambda j: (layer,) + (0,) * len(shape)))
    win = pl.BlockSpec((None, d, N_TILE), ix(lambda j: (layer, 0, jnp.minimum(j, n_in - 1))))
    wout = pl.BlockSpec((None, d_a + d_b, N_TILE), ix(lambda j: (layer, 0, ocol(j))))

    n_tiles = batch * tiles
    mixed = lambda s: jnp.maximum(s - 1, 0) // tiles
    resident = lambda *shape: pl.BlockSpec((None,) + shape, lambda s: (layer,) + (0,) * len(shape),
                                           pipeline_mode=pl.Buffered(1))
    x_new, conv_p, h_p, v_p = pl.pallas_call(
        functools.partial(_mix_prompt_body, tiles),
        grid=(n_tiles + 2,),
        in_specs=[
            pl.BlockSpec((MIX_ROWS, d), lambda s: (jnp.minimum(s, n_tiles - 1), 0)),
            vec(d), resident(d, d_in), resident(d_a + d_b, d), vec(d_a),
            full(n_a, CHUNK, CHUNK), full(CHUNK, n_a), full(CONV_W, d_b), vec(d_b),
            full(n_b, HEAD_DIM, 2 * HEAD_DIM), vec(d_b), vec(d_b), vec(d_b), vec(d_a + d_b),
        ],
        out_specs=[
            pl.BlockSpec((MIX_ROWS, d), lambda s: (jnp.maximum(s - 2, 0), 0)),
            pl.BlockSpec((None, CONV_W - 1, d_b), lambda s: (mixed(s), 0, 0)),
            pl.BlockSpec((None, 1, d_b), lambda s: (mixed(s), 0, 0)),
            pl.BlockSpec((None, CHUNK, d_a), lambda s: (mixed(s), 0, 0)),
        ],
        out_shape=[
            jax.ShapeDtypeStruct((total_rows, d), F32),
            jax.ShapeDtypeStruct((batch + 1, CONV_W - 1, d_b), F32),
            jax.ShapeDtypeStruct((batch + 1, 1, d_b), F32),
            jax.ShapeDtypeStruct((batch + 1, CHUNK, d_a), F32),
        ],
        scratch_shapes=[
            pltpu.VMEM((2, MIX_ROWS, d_in), F32),
            pltpu.VMEM((3, MIX_ROWS, d), F32),
            pltpu.VMEM((2, MIX_ROWS, d_a + d_b), BF16),
            pltpu.VMEM((MIX_ROWS + 2 * V7X_SUBLANES, d_b), F32),
            pltpu.VMEM((MIX_ROWS, d_b), F32),
            pltpu.VMEM((MIX_ROWS, d_b), F32),
            pltpu.VMEM((MIX_ROWS, d_a), F32),
            pltpu.VMEM((V7X_SUBLANES, d_b), F32),
        ],
        compiler_params=_params("arbitrary"),
        name="mix_prompt",
    )(x, gm, w_in, w_out, vg, ws, bs_t, cw, cb, wri, br, bi, lam, on)

    x_new, conv_s, h_s, v_s = pl.pallas_call(
        _mix_sample_body,
        grid=(steps,),
        in_specs=[
            pl.BlockSpec((s_rows, d), lambda j: (s_blk, 0)),
            pl.BlockSpec(memory_space=pl.ANY),
            vec(d), win, wout,
            full(CONV_W - 1, s_rows, d_b), full(s_rows, d_b),
            vec(d_a), vec(d_a), vec(d_a), full(CONV_W, d_b), vec(d_b),
            full(n_b, HEAD_DIM, 2 * HEAD_DIM), vec(d_b), vec(d_b), vec(d_b), vec(d_a + d_b),
        ],
        out_specs=[
            pl.BlockSpec((s_rows, N_TILE), lambda j: (s_blk, ocol(j))),
            pl.BlockSpec((CONV_W - 1, s_rows, d_b), lambda j: (0, 0, 0)),
            pl.BlockSpec((s_rows, d_b), lambda j: (0, 0)),
            pl.BlockSpec((s_rows, d_a), lambda j: (0, 0)),
        ],
        out_shape=[
            jax.ShapeDtypeStruct((total_rows, d), F32),
            jax.ShapeDtypeStruct((CONV_W - 1, s_rows, d_b), F32),
            jax.ShapeDtypeStruct((s_rows, d_b), F32),
            jax.ShapeDtypeStruct((s_rows, d_a), F32),
        ],
        scratch_shapes=[
            pltpu.VMEM((s_rows, d), BF16),
            pltpu.VMEM((n_in, s_rows, N_TILE), F32),
            pltpu.VMEM((s_rows, d_a + d_b), BF16),
            pltpu.VMEM((s_rows, d_a), F32),
        ],
        input_output_aliases={1: 0},
        compiler_params=_params("arbitrary"),
        name="mix_sample",
    )(x, x_new, gm, w_in, w_out, sc_t, h0, vg, ws0, bs0, cw, cb, wri, br, bi, lam, on)
    return x_new, conv_p[:batch], h_p[:batch], v_p[:batch], conv_s, h_s, v_s


def kernel(x_prompt, x_sample, state_conv, state_h, ffn1_norm, ffn1_wg, ffn1_wu, ffn1_wd, mix_norm, w_in, v_norm,
           w_spatial, b_spatial, conv_w, conv_b, w_rgate, b_rgate, w_igate, b_igate, lru_lambda, out_norm, w_out,
           ffn2_norm, ffn2_wg, ffn2_wu, ffn2_wd, final_norm):
    batch, seq, d_model = x_prompt.shape
    dec_batch, dec_seq, _ = x_sample.shape
    depth = w_in.shape[0]
    assert dec_seq == 1 and seq % MIX_ROWS == 0 and MIX_ROWS % CHUNK == 0 and (batch * seq) % NORM_ROWS == 0
    p_rows = batch * seq
    rows = p_rows + dec_batch
    assert rows % (ROW_TILES * 2 * V7X_SUBLANES) == 0 and p_rows % dec_batch == 0

    x = jnp.concatenate([x_prompt.reshape(p_rows, d_model), x_sample.reshape(dec_batch, d_model)], axis=0)

    row3 = lambda a: a.reshape(a.shape[0], 1, a.shape[-1])
    bf = lambda a: a.astype(BF16)
    g1, gm, g2 = row3(ffn1_norm), row3(mix_norm), row3(ffn2_norm)
    w_in_b, w_out_b = bf(w_in), bf(w_out)
    wri = bf(jnp.concatenate([w_rgate, w_igate], axis=-1))
    vg, cb, br, bi, lam, on = (row3(a) for a in (v_norm, conv_b, b_rgate, b_igate, lru_lambda, out_norm))
    bs_t = jnp.transpose(b_spatial, (0, 2, 1))
    ws0 = row3(jnp.repeat(w_spatial[:, :, 0, 0], HEAD_DIM, axis=-1))
    bs0 = row3(jnp.repeat(b_spatial[:, :, 0], HEAD_DIM, axis=-1))
    sc_t = jnp.transpose(state_conv, (0, 2, 1, 3))

    conv_p, h_p, v_p, conv_s, h_s, v_s = [], [], [], [], [], []
    for l in range(depth):
        x = _ffn(x, g1, ffn1_wg, ffn1_wu, ffn1_wd, l)
        x, cp, hp, vp, cs, hs, vs = _mixer_layer(x, batch, seq, l, gm, w_in_b, w_out_b, sc_t, state_h, vg,
                                                 w_spatial, bs_t, ws0, bs0, conv_w, cb, wri, br, bi, lam, on)
        x = _ffn(x, g2, ffn2_wg, ffn2_wu, ffn2_wd, l)
        conv_p.append(cp)
        h_p.append(hp.reshape(batch, -1))
        v_p.append(vp)
        conv_s.append(jnp.transpose(cs, (1, 0, 2)))
        h_s.append(hs)
        v_s.append(vs.reshape(dec_batch, dec_seq, -1))

    gf = final_norm.reshape(1, d_model)
    y_prompt = _final_norm(x, gf, NORM_ROWS, 0, p_rows // NORM_ROWS).reshape(batch, seq, d_model)
    y_sample = _final_norm(x, gf, dec_batch, p_rows // dec_batch, 1).reshape(dec_batch, dec_seq, d_model)
    return (y_prompt, y_sample, jnp.stack(conv_p), jnp.stack(h_p), jnp.stack(v_p),
            jnp.stack(conv_s), jnp.stack(h_s), jnp.stack(v_s))
```

```python
import functools

import jax
import jax.numpy as jnp
from jax import lax
from jax.experimental import pallas as pl
from jax.experimental.pallas import tpu as pltpu

F32 = jnp.float32
BF16 = jnp.bfloat16

HEAD_DIM = 128
CHUNK = 128
CONV_W = 4
LRU_C = 8.0
EPS = 1e-6

V7X_SUBLANES = 8
V7X_VMEM_LIMIT_BYTES = 56 * 1024 * 1024

ROW_TILES = 8
FF_TILE = 512
FF_TILE_HEAD = 256
N_TILE = 512
MIX_ROWS = 256
NORM_ROWS = 1024
PROJ_CHUNK = 256
MLP_HEAD_CHUNKS = (1, 1, 1, 1, 1, 1, 1, 1)
LRU_HEAD_CHUNKS = (2, 2, 2, 2, 2, 2, 2, 2)


def _params(*sem):
    return pltpu.CompilerParams(dimension_semantics=sem, vmem_limit_bytes=V7X_VMEM_LIMIT_BYTES)


def _rms(x, g):
    r = lax.rsqrt(jnp.mean(x * x, axis=-1, keepdims=True) + EPS)
    return x * r * g


def _gelu_tanh(x):
    c = 0.7978845608028654
    return x * (0.5 * (1.0 + jnp.tanh(c * (x + 0.044715 * (x * x * x)))))


def _ffn_step(first, x_ref, g_ref, weights, o_ref, xn_ref):
    @pl.when(first)
    def _():
        x = x_ref[...]
        xn_ref[...] = _rms(x, g_ref[...]).astype(BF16)
        o_ref[...] = x

    wg, wu, wd = weights()
    xn = xn_ref[...]
    gate = jnp.dot(xn, wg, preferred_element_type=F32)
    up = jnp.dot(xn, wu, preferred_element_type=F32)
    h = (gate * jax.nn.sigmoid(gate) * up * 0.5).astype(BF16)
    o_ref[...] += jnp.dot(h, wd, preferred_element_type=F32)


def _ffn_head_body(x_ref, g_ref, wg_ref, wu_ref, wd_ref, o_ref, wgb_ref, wub_ref, wdb_ref, xn_ref):
    def weights():
        rounded = []
        for src, dst in ((wg_ref, wgb_ref), (wu_ref, wub_ref), (wd_ref, wdb_ref)):
            w = src[...].astype(BF16)
            dst[...] = w
            rounded.append(w)
        return rounded

    _ffn_step(pl.program_id(0) == 0, x_ref, g_ref, weights, o_ref, xn_ref)


def _ffn_tail_body(x_ref, g_ref, wg_ref, wu_ref, wd_ref, o_in_ref, o_ref, xn_ref):
    del o_in_ref
    _ffn_step(pl.program_id(1) == 0, x_ref, g_ref, lambda: (wg_ref[...], wu_ref[...], wd_ref[...]), o_ref, xn_ref)


def _ffn(x, gain, wg, wu, wd, layer):
    rows, d = x.shape
    d_ff = wg.shape[-1]
    bm = rows // ROW_TILES
    gain_spec = lambda nd: pl.BlockSpec((None, 1, d), (lambda f: (layer, 0, 0)) if nd == 1
                                        else (lambda i, f: (layer, 0, 0)))
    out, wgb, wub, wdb = pl.pallas_call(
        _ffn_head_body,
        grid=(d_ff // FF_TILE_HEAD,),
        in_specs=[
            pl.BlockSpec((bm, d), lambda f: (0, 0), pipeline_mode=pl.Buffered(1)),
            gain_spec(1),
            pl.BlockSpec((None, d, FF_TILE_HEAD), lambda f: (layer, 0, f)),
            pl.BlockSpec((None, d, FF_TILE_HEAD), lambda f: (layer, 0, f)),
            pl.BlockSpec((None, FF_TILE_HEAD, d), lambda f: (layer, f, 0)),
        ],
        out_specs=[
            pl.BlockSpec((bm, d), lambda f: (0, 0)),
            pl.BlockSpec((d, FF_TILE_HEAD), lambda f: (0, f)),
            pl.BlockSpec((d, FF_TILE_HEAD), lambda f: (0, f)),
            pl.BlockSpec((FF_TILE_HEAD, d), lambda f: (f, 0)),
        ],
        out_shape=[
            jax.ShapeDtypeStruct((rows, d), F32),
            jax.ShapeDtypeStruct((d, d_ff), BF16),
            jax.ShapeDtypeStruct((d, d_ff), BF16),
            jax.ShapeDtypeStruct((d_ff, d), BF16),
        ],
        scratch_shapes=[pltpu.VMEM((bm, d), BF16)],
        compiler_params=_params("arbitrary"),
        name="ffn_head",
    )(x, gain, wg, wu, wd)
    return pl.pallas_call(
        _ffn_tail_body,
        grid=(ROW_TILES - 1, d_ff // FF_TILE),
        in_specs=[
            pl.BlockSpec((bm, d), lambda i, f: (i + 1, 0)),
            gain_spec(2),
            pl.BlockSpec((d, FF_TILE), lambda i, f: (0, f)),
            pl.BlockSpec((d, FF_TILE), lambda i, f: (0, f)),
            pl.BlockSpec((FF_TILE, d), lambda i, f: (f, 0)),
            pl.BlockSpec(memory_space=pl.ANY),
        ],
        out_specs=pl.BlockSpec((bm, d), lambda i, f: (i + 1, 0)),
        out_shape=jax.ShapeDtypeStruct((rows, d), F32),
        scratch_shapes=[pltpu.VMEM((bm, d), BF16)],
        input_output_aliases={5: 0},
        compiler_params=_params("arbitrary", "arbitrary"),
        name="ffn_tail",
    )(x, gain, wgb, wub, wdb, out)


def _norm_body(x_ref, g_ref, o_ref):
    o_ref[...] = _rms(x_ref[...], g_ref[...])


def _final_norm(x, gain, rows_per_block, first_block, n_blocks):
    d = x.shape[-1]
    return pl.pallas_call(
        _norm_body,
        grid=(n_blocks,),
        in_specs=[
            pl.BlockSpec((rows_per_block, d), lambda i: (first_block + i, 0)),
            pl.BlockSpec((1, d), lambda i: (0, 0)),
        ],
        out_specs=pl.BlockSpec((rows_per_block, d), lambda i: (i, 0)),
        out_shape=jax.ShapeDtypeStruct((rows_per_block * n_blocks, d), F32),
        compiler_params=_params("arbitrary"),
        name="final_norm",
    )(x, gain)


def _proj_steps(j, n_in, x_ref, gm_ref, win_ref, wout_ref, o_ref, winb_ref, woutb_ref, xn_scr, p_scr, mg_scr, mix_fn):
    n_out = x_ref.shape[-1] // N_TILE

    @pl.when(j == 0)
    def _():
        xn_scr[...] = _rms(x_ref[...], gm_ref[...]).astype(BF16)

    @pl.when(j < n_in)
    def _():
        w = win_ref[...].astype(BF16)
        winb_ref[...] = w
        p_scr[j] = jnp.dot(xn_scr[...], w, preferred_element_type=F32)

    pl.when(j == n_in)(mix_fn)

    for n in range(n_out):
        @pl.when(j == n_in + 1 + n)
        def _():
            w = wout_ref[...].astype(BF16)
            woutb_ref[...] = w
            o_ref[...] = x_ref[:, n * N_TILE:(n + 1) * N_TILE] + jnp.dot(mg_scr[...], w, preferred_element_type=F32)


def _pcol(p_scr, col):
    return p_scr[col // N_TILE, :, col % N_TILE:col % N_TILE + HEAD_DIM]


def _lru_gates(xconv, wri, b_r, b_i, c_lam):
    ri = jnp.dot(xconv.astype(BF16), wri, preferred_element_type=F32)
    r = jax.nn.sigmoid(ri[:, :HEAD_DIM] + b_r)
    i = jax.nn.sigmoid(ri[:, HEAD_DIM:] + b_i)
    log_a = c_lam * r
    a = jnp.exp(log_a)
    mult = jnp.sqrt(-jnp.tanh(log_a) * (a * a + 1.0))
    return a, mult * (i * xconv)


def _mix_prompt_body(tiles, x_ref, o_in_ref, gm_ref, win_ref, wout_ref, vg_ref, ws_ref, bs_ref, cw_ref, cb_ref,
                     wri_ref, br_ref, bi_ref, lam_ref, on_ref,
                     o_ref, conv_ref, h_ref, v_ref,
                     p_scr, xs_scr, mg_scr, xc_scr, a_scr, b_scr, ao_scr, hc_scr):
    del o_in_ref
    rows = x_ref.shape[0]
    d_a = ws_ref.shape[0] * HEAD_DIM
    d_b = wri_ref.shape[0] * HEAD_DIM
    n_grp = rows // V7X_SUBLANES
    lead = V7X_SUBLANES
    s = pl.program_id(0)
    p_new, p_mix = s % 2, (s + 1) % 2
    x_new, x_out = s % 3, (s + 1) % 3
    m_mix, m_out = (s + 1) % 2, s % 2
    first_tile = (jnp.maximum(s - 1, 0) % tiles) == 0

    @pl.when(s == 0)
    def _():
        p_scr[1] = jnp.zeros(p_scr.shape[1:], F32)
        xs_scr[1] = jnp.zeros(xs_scr.shape[1:], F32)
        xs_scr[2] = jnp.zeros(xs_scr.shape[1:], F32)
        mg_scr[0] = jnp.zeros(mg_scr.shape[1:], BF16)
        xc_scr[0:lead, :] = jnp.zeros((lead, d_b), F32)
        hc_scr[...] = jnp.zeros_like(hc_scr)

    x = x_ref[...]
    xs_scr[x_new] = x
    xn = _rms(x, gm_ref[...]).astype(BF16)
    mg_out = mg_scr[m_out]
    matmuls = []
    for k in range(wout_ref.shape[-1] // PROJ_CHUNK):
        def out_chunk(cs=pl.ds(k * PROJ_CHUNK, PROJ_CHUNK)):
            o_ref[:, cs] = xs_scr[x_out, :, cs] + jnp.dot(mg_out, wout_ref[:, cs], preferred_element_type=F32)
        matmuls.append(out_chunk)
    for k in range(win_ref.shape[-1] // PROJ_CHUNK):
        def in_chunk(cs=pl.ds(k * PROJ_CHUNK, PROJ_CHUNK)):
            p_scr[p_new, :, cs] = jnp.dot(xn, win_ref[:, cs], preferred_element_type=F32)
        matmuls.append(in_chunk)
    matmuls = iter(matmuls)

    def project(n_chunks):
        for _ in range(n_chunks):
            next(matmuls)()

    pcol = lambda col: p_scr[p_mix, :, col:col + HEAD_DIM]

    rr = lax.broadcasted_iota(jnp.int32, (CHUNK, CHUNK), 0)
    cc = lax.broadcasted_iota(jnp.int32, (CHUNK, CHUNK), 1)
    tril = (cc <= rr).astype(F32)
    for h in range(d_a // HEAD_DIM):
        project(MLP_HEAD_CHUNKS[h])
        cols = slice(h * HEAD_DIM, (h + 1) * HEAD_DIM)
        vh = _rms(pcol(d_a + h * HEAD_DIM), vg_ref[:, cols])
        v_ref[:, cols] = vh[rows - CHUNK:, :]
        vhb = vh.astype(BF16)
        wsm = (ws_ref[h] * tril).astype(BF16)
        bcol = bs_ref[:, h:h + 1]
        u = pcol(h * HEAD_DIM)
        for c in range(rows // CHUNK):
            rs = slice(c * CHUNK, (c + 1) * CHUNK)
            z = jnp.dot(wsm, vhb[rs, :], preferred_element_type=F32) + bcol
            ao_scr[rs, cols] = u[rs, :] * z
    mg_scr[m_mix, :, :d_a] = _rms(ao_scr[...], on_ref[:, :d_a]).astype(BF16)

    xc_scr[0:lead, :] = jnp.where(first_tile, 0.0, xc_scr[0:lead, :])
    for h in range(d_b // HEAD_DIM):
        cols = slice(h * HEAD_DIM, (h + 1) * HEAD_DIM)
        xc_scr[lead:lead + rows, cols] = pcol(2 * d_a + h * HEAD_DIM)
    conv_ref[...] = xc_scr[lead + rows - (CONV_W - 1):lead + rows, :]
    c_lam = -LRU_C * jax.nn.softplus(-lam_ref[...])
    sub = lax.broadcasted_iota(jnp.int32, (n_grp, V7X_SUBLANES, HEAD_DIM), 1)
    for h in range(d_b // HEAD_DIM):
        project(LRU_HEAD_CHUNKS[h])
        cols = slice(h * HEAD_DIM, (h + 1) * HEAD_DIM)
        acc = cw_ref[0:1, cols] * xc_scr[lead - 3:lead - 3 + rows, cols]
        for k in range(1, CONV_W):
            acc = acc + cw_ref[k:k + 1, cols] * xc_scr[lead - 3 + k:lead - 3 + k + rows, cols]
        xconv = cb_ref[:, cols] + acc
        a, bx = _lru_gates(xconv, wri_ref[h], br_ref[:, cols], bi_ref[:, cols], c_lam[:, cols])
        a3 = a.reshape(n_grp, V7X_SUBLANES, HEAD_DIM)
        b3 = bx.reshape(n_grp, V7X_SUBLANES, HEAD_DIM)
        for k in (1, 2, 4):
            a_prev = pltpu.roll(a3, k, axis=1)
            b_prev = pltpu.roll(b3, k, axis=1)
            keep = sub >= k
            b3 = jnp.where(keep, b3 + a3 * b_prev, b3)
            a3 = jnp.where(keep, a3 * a_prev, a3)
        a_scr[:, cols] = a3.reshape(rows, HEAD_DIM)
        b_scr[:, cols] = b3.reshape(rows, HEAD_DIM)
    xc_scr[0:lead, :] = xc_scr[rows:rows + lead, :]

    h_prev = jnp.where(first_tile, 0.0, hc_scr[...])
    for g in range(n_grp):
        gs = slice(g * V7X_SUBLANES, (g + 1) * V7X_SUBLANES)
        hs = b_scr[gs, :] + a_scr[gs, :] * h_prev
        b_scr[gs, :] = hs
        h_prev = jnp.broadcast_to(hs[V7X_SUBLANES - 1:, :], (V7X_SUBLANES, d_b))
    hc_scr[...] = h_prev
    h_ref[...] = h_prev[0:1, :]

    for h in range(d_b // HEAD_DIM):
        cols = slice(h * HEAD_DIM, (h + 1) * HEAD_DIM)
        ao_scr[:, cols] = b_scr[:, cols] * _gelu_tanh(pcol(2 * d_a + d_b + h * HEAD_DIM))
    mg_scr[m_mix, :, d_a:] = _rms(ao_scr[...], on_ref[:, d_a:]).astype(BF16)
    assert next(matmuls, None) is None


def _mix_sample_body(x_ref, gm_ref, win_ref, wout_ref, sc_ref, h0_ref, vg_ref, ws0_ref, bs0_ref, cw_ref,
                     cb_ref, wri_ref, br_ref, bi_ref, lam_ref, on_ref,
                     o_ref, conv_ref, h_ref, v_ref, winb_ref, woutb_ref,
                     xn_scr, p_scr, mg_scr, y_scr):
    d_a = vg_ref.shape[-1]
    d_b = wri_ref.shape[0] * HEAD_DIM

    def mix():
        for h in range(d_a // HEAD_DIM):
            cols = slice(h * HEAD_DIM, (h + 1) * HEAD_DIM)
            vh = _rms(_pcol(p_scr, d_a + h * HEAD_DIM), vg_ref[:, cols])
            v_ref[:, cols] = vh
            y_scr[:, cols] = _pcol(p_scr, h * HEAD_DIM) * (ws0_ref[:, cols] * vh + bs0_ref[:, cols])
        mg_scr[:, :d_a] = _rms(y_scr[...], on_ref[:, :d_a]).astype(BF16)

        for k in range(CONV_W - 2):
            conv_ref[k] = sc_ref[k + 1]
        c_lam = -LRU_C * jax.nn.softplus(-lam_ref[...])
        for h in range(d_b // HEAD_DIM):
            cols = slice(h * HEAD_DIM, (h + 1) * HEAD_DIM)
            xb = _pcol(p_scr, 2 * d_a + h * HEAD_DIM)
            conv_ref[CONV_W - 2, :, cols] = xb
            acc = cw_ref[0:1, cols] * sc_ref[0, :, cols]
            for k in range(1, CONV_W - 1):
                acc = acc + cw_ref[k:k + 1, cols] * sc_ref[k, :, cols]
            acc = acc + cw_ref[CONV_W - 1:CONV_W, cols] * xb
            xconv = cb_ref[:, cols] + acc
            a, bx = _lru_gates(xconv, wri_ref[h], br_ref[:, cols], bi_ref[:, cols], c_lam[:, cols])
            hs = a * h0_ref[:, cols] + bx
            h_ref[:, cols] = hs
            y_scr[:, cols] = hs * _gelu_tanh(_pcol(p_scr, 2 * d_a + d_b + h * HEAD_DIM))
        mg_scr[:, d_a:] = _rms(y_scr[...], on_ref[:, d_a:]).astype(BF16)

    _proj_steps(pl.program_id(0), p_scr.shape[0], x_ref, gm_ref, win_ref, wout_ref, o_ref, winb_ref, woutb_ref,
                xn_scr, p_scr, mg_scr, mix)


def _mixer_layer(x, batch, seq, layer, gm, w_in, w_out, sc_t, h0, vg, ws, bs_t, ws0, bs0, cw, cb, wri, br, bi, lam,
                 on):
    total_rows, d = x.shape
    d_in = w_in.shape[-1]
    n_a, n_b = ws.shape[1], wri.shape[1]
    d_a, d_b = n_a * HEAD_DIM, n_b * HEAD_DIM
    n_in, n_out = d_in // N_TILE, d // N_TILE
    steps = n_in + 1 + n_out
    tiles = seq // MIX_ROWS
    s_rows = total_rows - batch * seq
    s_blk = (batch * seq) // s_rows

    ix = lambda fn: (lambda *g: fn(g[-1]))
    ocol = lambda j: jnp.clip(j - n_in - 1, 0, n_out - 1)
    vec = lambda n: pl.BlockSpec((None, 1, n), ix(lambda j: (layer, 0, 0)))
    full = lambda *shape: pl.BlockSpec((None,) + shape, ix(lambda j: (layer,) + (0,) * len(shape)))
    win = pl.BlockSpec((None, d, N_TILE), ix(lambda j: (layer, 0, jnp.minimum(j, n_in - 1))))
    wout = pl.BlockSpec((None, d_a + d_b, N_TILE), ix(lambda j: (layer, 0, ocol(j))))

    x_new, conv_s, h_s, v_s, w_in_b, w_out_b = pl.pallas_call(
        _mix_sample_body,
        grid=(steps,),
        in_specs=[
            pl.BlockSpec((s_rows, d), lambda j: (s_blk, 0)),
            vec(d), win, wout,
            full(CONV_W - 1, s_rows, d_b), full(s_rows, d_b),
            vec(d_a), vec(d_a), vec(d_a), full(CONV_W, d_b), vec(d_b),
            full(n_b, HEAD_DIM, 2 * HEAD_DIM), vec(d_b), vec(d_b), vec(d_b), vec(d_a + d_b),
        ],
        out_specs=[
            pl.BlockSpec((s_rows, N_TILE), lambda j: (s_blk, ocol(j))),
            pl.BlockSpec((CONV_W - 1, s_rows, d_b), lambda j: (0, 0, 0)),
            pl.BlockSpec((s_rows, d_b), lambda j: (0, 0)),
            pl.BlockSpec((s_rows, d_a), lambda j: (0, 0)),
            pl.BlockSpec((d, N_TILE), lambda j: (0, jnp.minimum(j, n_in - 1))),
            pl.BlockSpec((d_a + d_b, N_TILE), lambda j: (0, ocol(j))),
        ],
        out_shape=[
            jax.ShapeDtypeStruct((total_rows, d), F32),
            jax.ShapeDtypeStruct((CONV_W - 1, s_rows, d_b), F32),
            jax.ShapeDtypeStruct((s_rows, d_b), F32),
            jax.ShapeDtypeStruct((s_rows, d_a), F32),
            jax.ShapeDtypeStruct((d, d_in), BF16),
            jax.ShapeDtypeStruct((d_a + d_b, d), BF16),
        ],
        scratch_shapes=[
            pltpu.VMEM((s_rows, d), BF16),
            pltpu.VMEM((n_in, s_rows, N_TILE), F32),
            pltpu.VMEM((s_rows, d_a + d_b), BF16),
            pltpu.VMEM((s_rows, d_a), F32),
        ],
        compiler_params=_params("arbitrary"),
        name="mix_sample",
    )(x, gm, w_in, w_out, sc_t, h0, vg, ws0, bs0, cw, cb, wri, br, bi, lam, on)

    n_tiles = batch * tiles
    mixed = lambda s: jnp.maximum(s - 1, 0) // tiles
    resident = lambda *shape: pl.BlockSpec(shape, lambda s: (0,) * len(shape))
    x_new, conv_p, h_p, v_p = pl.pallas_call(
        functools.partial(_mix_prompt_body, tiles),
        grid=(n_tiles + 2,),
        in_specs=[
            pl.BlockSpec((MIX_ROWS, d), lambda s: (jnp.minimum(s, n_tiles - 1), 0)),
            pl.BlockSpec(memory_space=pl.ANY),
            vec(d), resident(d, d_in), resident(d_a + d_b, d), vec(d_a),
            full(n_a, CHUNK, CHUNK), full(CHUNK, n_a), full(CONV_W, d_b), vec(d_b),
            full(n_b, HEAD_DIM, 2 * HEAD_DIM), vec(d_b), vec(d_b), vec(d_b), vec(d_a + d_b),
        ],
        out_specs=[
            pl.BlockSpec((MIX_ROWS, d), lambda s: (jnp.maximum(s - 2, 0), 0)),
            pl.BlockSpec((None, CONV_W - 1, d_b), lambda s: (mixed(s), 0, 0)),
            pl.BlockSpec((None, 1, d_b), lambda s: (mixed(s), 0, 0)),
            pl.BlockSpec((None, CHUNK, d_a), lambda s: (mixed(s), 0, 0)),
        ],
        out_shape=[
            jax.ShapeDtypeStruct((total_rows, d), F32),
            jax.ShapeDtypeStruct((batch + 1, CONV_W - 1, d_b), F32),
            jax.ShapeDtypeStruct((batch + 1, 1, d_b), F32),
            jax.ShapeDtypeStruct((batch + 1, CHUNK, d_a), F32),
        ],
        scratch_shapes=[
            pltpu.VMEM((2, MIX_ROWS, d_in), F32),
            pltpu.VMEM((3, MIX_ROWS, d), F32),
            pltpu.VMEM((2, MIX_ROWS, d_a + d_b), BF16),
            pltpu.VMEM((MIX_ROWS + 2 * V7X_SUBLANES, d_b), F32),
            pltpu.VMEM((MIX_ROWS, d_b), F32),
            pltpu.VMEM((MIX_ROWS, d_b), F32),
            pltpu.VMEM((MIX_ROWS, d_a), F32),
            pltpu.VMEM((V7X_SUBLANES, d_b), F32),
        ],
        input_output_aliases={1: 0},
        compiler_params=_params("arbitrary"),
        name="mix_prompt",
    )(x, x_new, gm, w_in_b, w_out_b, vg, ws, bs_t, cw, cb, wri, br, bi, lam, on)

    return x_new, conv_p[:batch], h_p[:batch], v_p[:batch], conv_s, h_s, v_s


def kernel(x_prompt, x_sample, state_conv, state_h, ffn1_norm, ffn1_wg, ffn1_wu, ffn1_wd, mix_norm, w_in, v_norm,
           w_spatial, b_spatial, conv_w, conv_b, w_rgate, b_rgate, w_igate, b_igate, lru_lambda, out_norm, w_out,
           ffn2_norm, ffn2_wg, ffn2_wu, ffn2_wd, final_norm):
    batch, seq, d_model = x_prompt.shape
    dec_batch, dec_seq, _ = x_sample.shape
    depth = w_in.shape[0]
    assert dec_seq == 1 and seq % MIX_ROWS == 0 and MIX_ROWS % CHUNK == 0 and (batch * seq) % NORM_ROWS == 0
    p_rows = batch * seq
    rows = p_rows + dec_batch
    assert rows % (ROW_TILES * 2 * V7X_SUBLANES) == 0 and p_rows % dec_batch == 0

    x = jnp.concatenate([x_prompt.reshape(p_rows, d_model), x_sample.reshape(dec_batch, d_model)], axis=0)

    row3 = lambda a: a.reshape(a.shape[0], 1, a.shape[-1])
    bf = lambda a: a.astype(BF16)
    g1, gm, g2 = row3(ffn1_norm), row3(mix_norm), row3(ffn2_norm)
    wri = bf(jnp.concatenate([w_rgate, w_igate], axis=-1))
    vg, cb, br, bi, lam, on = (row3(a) for a in (v_norm, conv_b, b_rgate, b_igate, lru_lambda, out_norm))
    bs_t = jnp.transpose(b_spatial, (0, 2, 1))
    ws0 = row3(jnp.repeat(w_spatial[:, :, 0, 0], HEAD_DIM, axis=-1))
    bs0 = row3(jnp.repeat(b_spatial[:, :, 0], HEAD_DIM, axis=-1))
    sc_t = jnp.transpose(state_conv, (0, 2, 1, 3))

    conv_p, h_p, v_p, conv_s, h_s, v_s = [], [], [], [], [], []
    for l in range(depth):
        x = _ffn(x, g1, ffn1_wg, ffn1_wu, ffn1_wd, l)
        x, cp, hp, vp, cs, hs, vs = _mixer_layer(x, batch, seq, l, gm, w_in, w_out, sc_t, state_h, vg,
                                                 w_spatial, bs_t, ws0, bs0, conv_w, cb, wri, br, bi, lam, on)
        x = _ffn(x, g2, ffn2_wg, ffn2_wu, ffn2_wd, l)
        conv_p.append(cp)
        h_p.append(hp.reshape(batch, -1))
        v_p.append(vp)
        conv_s.append(jnp.transpose(cs, (1, 0, 2)))
        h_s.append(hs)
        v_s.append(vs.reshape(dec_batch, dec_seq, -1))

    gf = final_norm.reshape(1, d_model)
    y_prompt = _final_norm(x, gf, NORM_ROWS, 0, p_rows // NORM_ROWS).reshape(batch, seq, d_model)
    y_sample = _final_norm(x, gf, dec_batch, p_rows // dec_batch, 1).reshape(dec_batch, dec_seq, d_model)
    return (y_prompt, y_sample, jnp.stack(conv_p), jnp.stack(h_p), jnp.stack(v_p),
            jnp.stack(conv_s), jnp.stack(h_s), jnp.stack(v_s))
```

```python
import functools

import jax
import jax.numpy as jnp
from jax import lax
from jax.experimental import pallas as pl
from jax.experimental.pallas import tpu as pltpu

F32 = jnp.float32
BF16 = jnp.bfloat16

HEAD_DIM = 128
CHUNK = 128
CONV_W = 4
LRU_C = 8.0
EPS = 1e-6

V7X_SUBLANES = 8
V7X_VMEM_LIMIT_BYTES = 56 * 1024 * 1024

ROW_TILES = 8
FF_TILE = 512
FF_TILE_HEAD = 256
N_TILE = 512
MIX_ROWS = 256
NORM_ROWS = 1024
PROJ_CHUNK = 256
MLP_HEAD_CHUNKS = (1, 1, 1, 1, 1, 1, 1, 1)
LRU_HEAD_CHUNKS = (2, 2, 2, 2, 2, 2, 2, 2)
TAIL_CHUNKS = (0, 0)


def _params(*sem):
    return pltpu.CompilerParams(dimension_semantics=sem, vmem_limit_bytes=V7X_VMEM_LIMIT_BYTES)


def _rms(x, g):
    r = lax.rsqrt(jnp.mean(x * x, axis=-1, keepdims=True) + EPS)
    return x * r * g


def _gelu_tanh(x):
    c = 0.7978845608028654
    return x * (0.5 * (1.0 + jnp.tanh(c * (x + 0.044715 * (x * x * x)))))


def _ffn_step(first, x_ref, g_ref, weights, o_ref, xn_ref):
    @pl.when(first)
    def _():
        x = x_ref[...]
        xn_ref[...] = _rms(x, g_ref[...]).astype(BF16)
        o_ref[...] = x

    wg, wu, wd = weights()
    xn = xn_ref[...]
    gate = jnp.dot(xn, wg, preferred_element_type=F32)
    up = jnp.dot(xn, wu, preferred_element_type=F32)
    h = (gate * jax.nn.sigmoid(gate) * up * 0.5).astype(BF16)
    o_ref[...] += jnp.dot(h, wd, preferred_element_type=F32)


def _ffn_head_body(x_ref, g_ref, wg_ref, wu_ref, wd_ref, o_ref, wgb_ref, wub_ref, wdb_ref, xn_ref):
    def weights():
        rounded = []
        for src, dst in ((wg_ref, wgb_ref), (wu_ref, wub_ref), (wd_ref, wdb_ref)):
            w = src[...].astype(BF16)
            dst[...] = w
            rounded.append(w)
        return rounded

    _ffn_step(pl.program_id(0) == 0, x_ref, g_ref, weights, o_ref, xn_ref)


def _ffn_tail_body(x_ref, g_ref, wg_ref, wu_ref, wd_ref, o_in_ref, o_ref, xn_ref):
    del o_in_ref
    cols = lambda ref: jnp.concatenate([ref[k] for k in range(ref.shape[0])], axis=-1)
    _ffn_step(pl.program_id(1) == 0, x_ref, g_ref, lambda: (cols(wg_ref), cols(wu_ref), wd_ref[...]), o_ref, xn_ref)


def _ffn(x, gain, wg, wu, wd, layer):
    rows, d = x.shape
    d_ff = wg.shape[-1]
    bm = rows // ROW_TILES
    gain_spec = lambda nd: pl.BlockSpec((None, 1, d), (lambda f: (layer, 0, 0)) if nd == 1
                                        else (lambda i, f: (layer, 0, 0)))
    out, wgb, wub, wdb = pl.pallas_call(
        _ffn_head_body,
        grid=(d_ff // FF_TILE_HEAD,),
        in_specs=[
            pl.BlockSpec((bm, d), lambda f: (0, 0), pipeline_mode=pl.Buffered(1)),
            gain_spec(1),
            pl.BlockSpec((None, d, FF_TILE_HEAD), lambda f: (layer, 0, f)),
            pl.BlockSpec((None, d, FF_TILE_HEAD), lambda f: (layer, 0, f)),
            pl.BlockSpec((None, FF_TILE_HEAD, d), lambda f: (layer, f, 0)),
        ],
        out_specs=[
            pl.BlockSpec((bm, d), lambda f: (0, 0)),
            pl.BlockSpec((None, d, FF_TILE_HEAD), lambda f: (f, 0, 0)),
            pl.BlockSpec((None, d, FF_TILE_HEAD), lambda f: (f, 0, 0)),
            pl.BlockSpec((FF_TILE_HEAD, d), lambda f: (f, 0)),
        ],
        out_shape=[
            jax.ShapeDtypeStruct((rows, d), F32),
            jax.ShapeDtypeStruct((d_ff // FF_TILE_HEAD, d, FF_TILE_HEAD), BF16),
            jax.ShapeDtypeStruct((d_ff // FF_TILE_HEAD, d, FF_TILE_HEAD), BF16),
            jax.ShapeDtypeStruct((d_ff, d), BF16),
        ],
        scratch_shapes=[pltpu.VMEM((bm, d), BF16)],
        compiler_params=_params("arbitrary"),
        name="ffn_head",
    )(x, gain, wg, wu, wd)
    return pl.pallas_call(
        _ffn_tail_body,
        grid=(ROW_TILES - 1, d_ff // FF_TILE),
        in_specs=[
            pl.BlockSpec((bm, d), lambda i, f: (i + 1, 0)),
            gain_spec(2),
            pl.BlockSpec((FF_TILE // FF_TILE_HEAD, d, FF_TILE_HEAD), lambda i, f: (f, 0, 0)),
            pl.BlockSpec((FF_TILE // FF_TILE_HEAD, d, FF_TILE_HEAD), lambda i, f: (f, 0, 0)),
            pl.BlockSpec((FF_TILE, d), lambda i, f: (f, 0)),
            pl.BlockSpec(memory_space=pl.ANY),
        ],
        out_specs=pl.BlockSpec((bm, d), lambda i, f: (i + 1, 0)),
        out_shape=jax.ShapeDtypeStruct((rows, d), F32),
        scratch_shapes=[pltpu.VMEM((bm, d), BF16)],
        input_output_aliases={5: 0},
        compiler_params=_params("arbitrary", "arbitrary"),
        name="ffn_tail",
    )(x, gain, wgb, wub, wdb, out)


def _norm_body(x_ref, g_ref, o_ref):
    o_ref[...] = _rms(x_ref[...], g_ref[...])


def _final_norm(x, gain, rows_per_block, first_block, n_blocks):
    d = x.shape[-1]
    return pl.pallas_call(
        _norm_body,
        grid=(n_blocks,),
        in_specs=[
            pl.BlockSpec((rows_per_block, d), lambda i: (first_block + i, 0)),
            pl.BlockSpec((1, d), lambda i: (0, 0)),
        ],
        out_specs=pl.BlockSpec((rows_per_block, d), lambda i: (i, 0)),
        out_shape=jax.ShapeDtypeStruct((rows_per_block * n_blocks, d), F32),
        compiler_params=_params("arbitrary"),
        name="final_norm",
    )(x, gain)


def _proj_steps(j, n_in, x_ref, gm_ref, win_ref, wout_ref, o_ref, winb_ref, woutb_ref, xn_scr, p_scr, mg_scr, mix_fn):
    n_out = x_ref.shape[-1] // N_TILE

    @pl.when(j == 0)
    def _():
        xn_scr[...] = _rms(x_ref[...], gm_ref[...]).astype(BF16)

    @pl.when(j < n_in)
    def _():
        w = win_ref[...].astype(BF16)
        winb_ref[...] = w
        p_scr[j] = jnp.dot(xn_scr[...], w, preferred_element_type=F32)

    pl.when(j == n_in)(mix_fn)

    for n in range(n_out):
        @pl.when(j == n_in + 1 + n)
        def _():
            w = wout_ref[...].astype(BF16)
            woutb_ref[...] = w
            o_ref[...] = x_ref[:, n * N_TILE:(n + 1) * N_TILE] + jnp.dot(mg_scr[...], w, preferred_element_type=F32)


def _pcol(p_scr, col):
    return p_scr[col // N_TILE, :, col % N_TILE:col % N_TILE + HEAD_DIM]


def _lru_gates(xconv, wri, b_r, b_i, c_lam):
    ri = jnp.dot(xconv.astype(BF16), wri, preferred_element_type=F32)
    r = jax.nn.sigmoid(ri[:, :HEAD_DIM] + b_r)
    i = jax.nn.sigmoid(ri[:, HEAD_DIM:] + b_i)
    log_a = c_lam * r
    a = jnp.exp(log_a)
    mult = jnp.sqrt(-jnp.tanh(log_a) * (a * a + 1.0))
    return a, mult * (i * xconv)


def _mix_prompt_body(tiles, x_ref, o_in_ref, gm_ref, win_ref, wout_ref, vg_ref, ws_ref, bs_ref, cw_ref, cb_ref,
                     wri_ref, br_ref, bi_ref, lam_ref, on_ref,
                     o_ref, conv_ref, h_ref, v_ref,
                     p_scr, xs_scr, mg_scr, xc_scr, a_scr, b_scr, ao_scr, hc_scr):
    del o_in_ref
    rows = x_ref.shape[0]
    d_a = ws_ref.shape[0] * HEAD_DIM
    d_b = wri_ref.shape[0] * HEAD_DIM
    n_grp = rows // V7X_SUBLANES
    lead = V7X_SUBLANES
    s = pl.program_id(0)
    p_new, p_mix = s % 2, (s + 1) % 2
    x_new, x_out = s % 3, (s + 1) % 3
    m_mix, m_out = (s + 1) % 2, s % 2
    first_tile = (jnp.maximum(s - 1, 0) % tiles) == 0

    @pl.when(s == 0)
    def _():
        p_scr[1] = jnp.zeros(p_scr.shape[1:], F32)
        xs_scr[1] = jnp.zeros(xs_scr.shape[1:], F32)
        xs_scr[2] = jnp.zeros(xs_scr.shape[1:], F32)
        mg_scr[0] = jnp.zeros(mg_scr.shape[1:], BF16)
        xc_scr[0:lead, :] = jnp.zeros((lead, d_b), F32)
        hc_scr[...] = jnp.zeros_like(hc_scr)

    x = x_ref[...]
    xs_scr[x_new] = x
    xn = _rms(x, gm_ref[...]).astype(BF16)
    mg_out = mg_scr[m_out]
    matmuls = []
    for k in range(wout_ref.shape[-1] // PROJ_CHUNK):
        def out_chunk(cs=pl.ds(k * PROJ_CHUNK, PROJ_CHUNK)):
            o_ref[:, cs] = xs_scr[x_out, :, cs] + jnp.dot(mg_out, wout_ref[:, cs], preferred_element_type=F32)
        matmuls.append(out_chunk)
    for k in range(win_ref.shape[-1] // PROJ_CHUNK):
        def in_chunk(first=k * (PROJ_CHUNK // HEAD_DIM), cs=pl.ds(k * PROJ_CHUNK, PROJ_CHUNK)):
            res = jnp.dot(xn, win_ref[:, cs], preferred_element_type=F32)
            for j in range(PROJ_CHUNK // HEAD_DIM):
                p_scr[p_new, first + j] = res[:, j * HEAD_DIM:(j + 1) * HEAD_DIM]
        matmuls.append(in_chunk)
    matmuls = iter(matmuls)

    def project(n_chunks):
        for _ in range(n_chunks):
            next(matmuls)()

    pcol = lambda col: p_scr[p_mix, col // HEAD_DIM]

    rr = lax.broadcasted_iota(jnp.int32, (CHUNK, CHUNK), 0)
    cc = lax.broadcasted_iota(jnp.int32, (CHUNK, CHUNK), 1)
    tril = (cc <= rr).astype(F32)
    for h in range(d_a // HEAD_DIM):
        project(MLP_HEAD_CHUNKS[h])
        cols = slice(h * HEAD_DIM, (h + 1) * HEAD_DIM)
        vh = _rms(pcol(d_a + h * HEAD_DIM), vg_ref[:, cols])
        v_ref[:, cols] = vh[rows - CHUNK:, :]
        vhb = vh.astype(BF16)
        wsm = (ws_ref[h] * tril).astype(BF16)
        bcol = bs_ref[:, h:h + 1]
        u = pcol(h * HEAD_DIM)
        for c in range(rows // CHUNK):
            rs = slice(c * CHUNK, (c + 1) * CHUNK)
            z = jnp.dot(wsm, vhb[rs, :], preferred_element_type=F32) + bcol
            ao_scr[rs, cols] = u[rs, :] * z
    mg_scr[m_mix, :, :d_a] = _rms(ao_scr[...], on_ref[:, :d_a]).astype(BF16)

    xc_scr[0:lead, :] = jnp.where(first_tile, 0.0, xc_scr[0:lead, :])
    for h in range(d_b // HEAD_DIM):
        cols = slice(h * HEAD_DIM, (h + 1) * HEAD_DIM)
        xc_scr[lead:lead + rows, cols] = pcol(2 * d_a + h * HEAD_DIM)
    conv_ref[...] = xc_scr[lead + rows - (CONV_W - 1):lead + rows, :]
    c_lam = -LRU_C * jax.nn.softplus(-lam_ref[...])
    sub = lax.broadcasted_iota(jnp.int32, (n_grp, V7X_SUBLANES, HEAD_DIM), 1)
    for h in range(d_b // HEAD_DIM):
        project(LRU_HEAD_CHUNKS[h])
        cols = slice(h * HEAD_DIM, (h + 1) * HEAD_DIM)
        acc = cw_ref[0:1, cols] * xc_scr[lead - 3:lead - 3 + rows, cols]
        for k in range(1, CONV_W):
            acc = acc + cw_ref[k:k + 1, cols] * xc_scr[lead - 3 + k:lead - 3 + k + rows, cols]
        xconv = cb_ref[:, cols] + acc
        a, bx = _lru_gates(xconv, wri_ref[h], br_ref[:, cols], bi_ref[:, cols], c_lam[:, cols])
        a3 = a.reshape(n_grp, V7X_SUBLANES, HEAD_DIM)
        b3 = bx.reshape(n_grp, V7X_SUBLANES, HEAD_DIM)
        for k in (1, 2, 4):
            a_prev = pltpu.roll(a3, k, axis=1)
            b_prev = pltpu.roll(b3, k, axis=1)
            keep = sub >= k
            b3 = jnp.where(keep, b3 + a3 * b_prev, b3)
            a3 = jnp.where(keep, a3 * a_prev, a3)
        a_scr[:, cols] = a3.reshape(rows, HEAD_DIM)
        b_scr[:, cols] = b3.reshape(rows, HEAD_DIM)
    xc_scr[0:lead, :] = xc_scr[rows:rows + lead, :]

    project(TAIL_CHUNKS[0])
    h_prev = jnp.where(first_tile, 0.0, hc_scr[...])
    for g in range(n_grp):
        gs = slice(g * V7X_SUBLANES, (g + 1) * V7X_SUBLANES)
        hs = b_scr[gs, :] + a_scr[gs, :] * h_prev
        b_scr[gs, :] = hs
        h_prev = jnp.broadcast_to(hs[V7X_SUBLANES - 1:, :], (V7X_SUBLANES, d_b))
    hc_scr[...] = h_prev
    h_ref[...] = h_prev[0:1, :]

    for h in range(d_b // HEAD_DIM):
        cols = slice(h * HEAD_DIM, (h + 1) * HEAD_DIM)
        ao_scr[:, cols] = b_scr[:, cols] * _gelu_tanh(pcol(2 * d_a + d_b + h * HEAD_DIM))
    mg_scr[m_mix, :, d_a:] = _rms(ao_scr[...], on_ref[:, d_a:]).astype(BF16)
    assert next(matmuls, None) is None


def _mix_sample_body(x_ref, gm_ref, win_ref, wout_ref, sc_ref, h0_ref, vg_ref, ws0_ref, bs0_ref, cw_ref,
                     cb_ref, wri_ref, br_ref, bi_ref, lam_ref, on_ref,
                     o_ref, conv_ref, h_ref, v_ref, winb_ref, woutb_ref,
                     xn_scr, p_scr, mg_scr, y_scr):
    d_a = vg_ref.shape[-1]
    d_b = wri_ref.shape[0] * HEAD_DIM

    def mix():
        for h in range(d_a // HEAD_DIM):
            cols = slice(h * HEAD_DIM, (h + 1) * HEAD_DIM)
            vh = _rms(_pcol(p_scr, d_a + h * HEAD_DIM), vg_ref[:, cols])
            v_ref[:, cols] = vh
            y_scr[:, cols] = _pcol(p_scr, h * HEAD_DIM) * (ws0_ref[:, cols] * vh + bs0_ref[:, cols])
        mg_scr[:, :d_a] = _rms(y_scr[...], on_ref[:, :d_a]).astype(BF16)

        for k in range(CONV_W - 2):
            conv_ref[k] = sc_ref[k + 1]
        c_lam = -LRU_C * jax.nn.softplus(-lam_ref[...])
        for h in range(d_b // HEAD_DIM):
            cols = slice(h * HEAD_DIM, (h + 1) * HEAD_DIM)
            xb = _pcol(p_scr, 2 * d_a + h * HEAD_DIM)
            conv_ref[CONV_W - 2, :, cols] = xb
            acc = cw_ref[0:1, cols] * sc_ref[0, :, cols]
            for k in range(1, CONV_W - 1):
                acc = acc + cw_ref[k:k + 1, cols] * sc_ref[k, :, cols]
            acc = acc + cw_ref[CONV_W - 1:CONV_W, cols] * xb
            xconv = cb_ref[:, cols] + acc
            a, bx = _lru_gates(xconv, wri_ref[h], br_ref[:, cols], bi_ref[:, cols], c_lam[:, cols])
            hs = a * h0_ref[:, cols] + bx
            h_ref[:, cols] = hs
            y_scr[:, cols] = hs * _gelu_tanh(_pcol(p_scr, 2 * d_a + d_b + h * HEAD_DIM))
        mg_scr[:, d_a:] = _rms(y_scr[...], on_ref[:, d_a:]).astype(BF16)

    _proj_steps(pl.program_id(0), p_scr.shape[0], x_ref, gm_ref, win_ref, wout_ref, o_ref, winb_ref, woutb_ref,
                xn_scr, p_scr, mg_scr, mix)


def _mixer_layer(x, batch, seq, layer, gm, w_in, w_out, sc_t, h0, vg, ws, bs_t, ws0, bs0, cw, cb, wri, br, bi, lam,
                 on):
    total_rows, d = x.shape
    d_in = w_in.shape[-1]
    n_a, n_b = ws.shape[1], wri.shape[1]
    d_a, d_b = n_a * HEAD_DIM, n_b * HEAD_DIM
    n_in, n_out = d_in // N_TILE, d // N_TILE
    steps = n_in + 1 + n_out
    tiles = seq // MIX_ROWS
    s_rows = total_rows - batch * seq
    s_blk = (batch * seq) // s_rows

    ix = lambda fn: (lambda *g: fn(g[-1]))
    ocol = lambda j: jnp.clip(j - n_in - 1, 0, n_out - 1)
    vec = lambda n: pl.BlockSpec((None, 1, n), ix(lambda j: (layer, 0, 0)))
    full = lambda *shape: pl.BlockSpec((None,) + shape, ix(lambda j: (layer,) + (0,) * len(shape)))
    win = pl.BlockSpec((None, d, N_TILE), ix(lambda j: (layer, 0, jnp.minimum(j, n_in - 1))))
    wout = pl.BlockSpec((None, d_a + d_b, N_TILE), ix(lambda j: (layer, 0, ocol(j))))

    x_new, conv_s, h_s, v_s, w_in_b, w_out_b = pl.pallas_call(
        _mix_sample_body,
        grid=(steps,),
        in_specs=[
            pl.BlockSpec((s_rows, d), lambda j: (s_blk, 0)),
            vec(d), win, wout,
            full(CONV_W - 1, s_rows, d_b), full(s_rows, d_b),
            vec(d_a), vec(d_a), vec(d_a), full(CONV_W, d_b), vec(d_b),
            full(n_b, HEAD_DIM, 2 * HEAD_DIM), vec(d_b), vec(d_b), vec(d_b), vec(d_a + d_b),
        ],
        out_specs=[
            pl.BlockSpec((s_rows, N_TILE), lambda j: (s_blk, ocol(j))),
            pl.BlockSpec((CONV_W - 1, s_rows, d_b), lambda j: (0, 0, 0)),
            pl.BlockSpec((s_rows, d_b), lambda j: (0, 0)),
            pl.BlockSpec((s_rows, d_a), lambda j: (0, 0)),
            pl.BlockSpec((d, N_TILE), lambda j: (0, jnp.minimum(j, n_in - 1))),
            pl.BlockSpec((d_a + d_b, N_TILE), lambda j: (0, ocol(j))),
        ],
        out_shape=[
            jax.ShapeDtypeStruct((total_rows, d), F32),
            jax.ShapeDtypeStruct((CONV_W - 1, s_rows, d_b), F32),
            jax.ShapeDtypeStruct((s_rows, d_b), F32),
            jax.ShapeDtypeStruct((s_rows, d_a), F32),
            jax.ShapeDtypeStruct((d, d_in), BF16),
            jax.ShapeDtypeStruct((d_a + d_b, d), BF16),
        ],
        scratch_shapes=[
            pltpu.VMEM((s_rows, d), BF16),
            pltpu.VMEM((n_in, s_rows, N_TILE), F32),
            pltpu.VMEM((s_rows, d_a + d_b), BF16),
            pltpu.VMEM((s_rows, d_a), F32),
        ],
        compiler_params=_params("arbitrary"),
        name="mix_sample",
    )(x, gm, w_in, w_out, sc_t, h0, vg, ws0, bs0, cw, cb, wri, br, bi, lam, on)

    n_tiles = batch * tiles
    mixed = lambda s: jnp.maximum(s - 1, 0) // tiles
    resident = lambda *shape: pl.BlockSpec(shape, lambda s: (0,) * len(shape))
    x_new, conv_p, h_p, v_p = pl.pallas_call(
        functools.partial(_mix_prompt_body, tiles),
        grid=(n_tiles + 2,),
        in_specs=[
            pl.BlockSpec((MIX_ROWS, d), lambda s: (jnp.minimum(s, n_tiles - 1), 0)),
            pl.BlockSpec(memory_space=pl.ANY),
            vec(d), resident(d, d_in), resident(d_a + d_b, d), vec(d_a),
            full(n_a, CHUNK, CHUNK), full(CHUNK, n_a), full(CONV_W, d_b), vec(d_b),
            full(n_b, HEAD_DIM, 2 * HEAD_DIM), vec(d_b), vec(d_b), vec(d_b), vec(d_a + d_b),
        ],
        out_specs=[
            pl.BlockSpec((MIX_ROWS, d), lambda s: (jnp.maximum(s - 2, 0), 0)),
            pl.BlockSpec((None, CONV_W - 1, d_b), lambda s: (mixed(s), 0, 0)),
            pl.BlockSpec((None, 1, d_b), lambda s: (mixed(s), 0, 0)),
            pl.BlockSpec((None, CHUNK, d_a), lambda s: (mixed(s), 0, 0)),
        ],
        out_shape=[
            jax.ShapeDtypeStruct((total_rows, d), F32),
            jax.ShapeDtypeStruct((batch + 1, CONV_W - 1, d_b), F32),
            jax.ShapeDtypeStruct((batch + 1, 1, d_b), F32),
            jax.ShapeDtypeStruct((batch + 1, CHUNK, d_a), F32),
        ],
        scratch_shapes=[
            pltpu.VMEM((2, d_in // HEAD_DIM, MIX_ROWS, HEAD_DIM), F32),
            pltpu.VMEM((3, MIX_ROWS, d), F32),
            pltpu.VMEM((2, MIX_ROWS, d_a + d_b), BF16),
            pltpu.VMEM((MIX_ROWS + 2 * V7X_SUBLANES, d_b), F32),
            pltpu.VMEM((MIX_ROWS, d_b), F32),
            pltpu.VMEM((MIX_ROWS, d_b), F32),
            pltpu.VMEM((MIX_ROWS, d_a), F32),
            pltpu.VMEM((V7X_SUBLANES, d_b), F32),
        ],
        input_output_aliases={1: 0},
        compiler_params=_params("arbitrary"),
        name="mix_prompt",
    )(x, x_new, gm, w_in_b, w_out_b, vg, ws, bs_t, cw, cb, wri, br, bi, lam, on)

    return x_new, conv_p[:batch], h_p[:batch], v_p[:batch], conv_s, h_s, v_s


def kernel(x_prompt, x_sample, state_conv, state_h, ffn1_norm, ffn1_wg, ffn1_wu, ffn1_wd, mix_norm, w_in, v_norm,
           w_spatial, b_spatial, conv_w, conv_b, w_rgate, b_rgate, w_igate, b_igate, lru_lambda, out_norm, w_out,
           ffn2_norm, ffn2_wg, ffn2_wu, ffn2_wd, final_norm):
    batch, seq, d_model = x_prompt.shape
    dec_batch, dec_seq, _ = x_sample.shape
    depth = w_in.shape[0]
    assert dec_seq == 1 and seq % MIX_ROWS == 0 and MIX_ROWS % CHUNK == 0 and (batch * seq) % NORM_ROWS == 0
    p_rows = batch * seq
    rows = p_rows + dec_batch
    assert rows % (ROW_TILES * 2 * V7X_SUBLANES) == 0 and p_rows % dec_batch == 0

    x = jnp.concatenate([x_prompt.reshape(p_rows, d_model), x_sample.reshape(dec_batch, d_model)], axis=0)

    row3 = lambda a: a.reshape(a.shape[0], 1, a.shape[-1])
    bf = lambda a: a.astype(BF16)
    g1, gm, g2 = row3(ffn1_norm), row3(mix_norm), row3(ffn2_norm)
    wri = bf(jnp.concatenate([w_rgate, w_igate], axis=-1))
    vg, cb, br, bi, lam, on = (row3(a) for a in (v_norm, conv_b, b_rgate, b_igate, lru_lambda, out_norm))
    bs_t = jnp.transpose(b_spatial, (0, 2, 1))
    ws0 = row3(jnp.repeat(w_spatial[:, :, 0, 0], HEAD_DIM, axis=-1))
    bs0 = row3(jnp.repeat(b_spatial[:, :, 0], HEAD_DIM, axis=-1))
    sc_t = jnp.transpose(state_conv, (0, 2, 1, 3))

    conv_p, h_p, v_p, conv_s, h_s, v_s = [], [], [], [], [], []
    for l in range(depth):
        x = _ffn(x, g1, ffn1_wg, ffn1_wu, ffn1_wd, l)
        x, cp, hp, vp, cs, hs, vs = _mixer_layer(x, batch, seq, l, gm, w_in, w_out, sc_t, state_h, vg,
                                                 w_spatial, bs_t, ws0, bs0, conv_w, cb, wri, br, bi, lam, on)
        x = _ffn(x, g2, ffn2_wg, ffn2_wu, ffn2_wd, l)
        conv_p.append(cp)
        h_p.append(hp.reshape(batch, -1))
        v_p.append(vp)
        conv_s.append(jnp.transpose(cs, (1, 0, 2)))
        h_s.append(hs)
        v_s.append(vs.reshape(dec_batch, dec_seq, -1))

    gf = final_norm.reshape(1, d_model)
    y_prompt = _final_norm(x, gf, NORM_ROWS, 0, p_rows // NORM_ROWS).reshape(batch, seq, d_model)
    y_sample = _final_norm(x, gf, dec_batch, p_rows // dec_batch, 1).reshape(dec_batch, dec_seq, d_model)
    return (y_prompt, y_sample, jnp.stack(conv_p), jnp.stack(h_p), jnp.stack(v_p),
            jnp.stack(conv_s), jnp.stack(h_s), jnp.stack(v_s))
```

```python
import functools

import jax
import jax.numpy as jnp
from jax import lax
from jax.experimental import pallas as pl
from jax.experimental.pallas import tpu as pltpu

F32 = jnp.float32
BF16 = jnp.bfloat16

HEAD_DIM = 128
CHUNK = 128
CONV_W = 4
LRU_C = 8.0
EPS = 1e-6

V7X_SUBLANES = 8
V7X_VMEM_LIMIT_BYTES = 56 * 1024 * 1024

ROW_TILES = 8
FF_TILE = 512
FF_TILE_HEAD = 256
N_TILE = 512
MIX_ROWS = 256
NORM_ROWS = 1024
PROJ_CHUNK = 256
MLP_HEAD_CHUNKS = (1, 1, 1, 1, 1, 1, 1, 1)
LRU_HEAD_CHUNKS = (2, 2, 2, 2, 2, 2, 2, 2)
LRU_ROW_SPLIT = 2
TAIL_CHUNKS = (0, 0)


def _params(*sem):
    return pltpu.CompilerParams(dimension_semantics=sem, vmem_limit_bytes=V7X_VMEM_LIMIT_BYTES)


def _rms(x, g):
    r = lax.rsqrt(jnp.mean(x * x, axis=-1, keepdims=True) + EPS)
    return x * r * g


def _gelu_tanh(x):
    c = 0.7978845608028654
    return x * (0.5 * (1.0 + jnp.tanh(c * (x + 0.044715 * (x * x * x)))))


def _ffn_step(first, x_ref, g_ref, weights, o_ref, xn_ref):
    @pl.when(first)
    def _():
        x = x_ref[...]
        xn_ref[...] = _rms(x, g_ref[...]).astype(BF16)
        o_ref[...] = x

    wg, wu, wd = weights()
    xn = xn_ref[...]
    gate = jnp.dot(xn, wg, preferred_element_type=F32)
    up = jnp.dot(xn, wu, preferred_element_type=F32)
    h = (gate * jax.nn.sigmoid(gate) * up * 0.5).astype(BF16)
    o_ref[...] += jnp.dot(h, wd, preferred_element_type=F32)


def _ffn_head_body(x_ref, g_ref, wg_ref, wu_ref, wd_ref, o_ref, wgb_ref, wub_ref, wdb_ref, xn_ref):
    def weights():
        rounded = []
        for src, dst in ((wg_ref, wgb_ref), (wu_ref, wub_ref), (wd_ref, wdb_ref)):
            w = src[...].astype(BF16)
            dst[...] = w
            rounded.append(w)
        return rounded

    _ffn_step(pl.program_id(0) == 0, x_ref, g_ref, weights, o_ref, xn_ref)


def _ffn_tail_body(x_ref, g_ref, wg_ref, wu_ref, wd_ref, o_in_ref, o_ref, xn_ref):
    del o_in_ref
    cols = lambda ref: jnp.concatenate([ref[k] for k in range(ref.shape[0])], axis=-1)
    _ffn_step(pl.program_id(1) == 0, x_ref, g_ref, lambda: (cols(wg_ref), cols(wu_ref), wd_ref[...]), o_ref, xn_ref)


def _ffn(x, gain, wg, wu, wd, layer):
    rows, d = x.shape
    d_ff = wg.shape[-1]
    bm = rows // ROW_TILES
    gain_spec = lambda nd: pl.BlockSpec((None, 1, d), (lambda f: (layer, 0, 0)) if nd == 1
                                        else (lambda i, f: (layer, 0, 0)))
    out, wgb, wub, wdb = pl.pallas_call(
        _ffn_head_body,
        grid=(d_ff // FF_TILE_HEAD,),
        in_specs=[
            pl.BlockSpec((bm, d), lambda f: (0, 0), pipeline_mode=pl.Buffered(1)),
            gain_spec(1),
            pl.BlockSpec((None, d, FF_TILE_HEAD), lambda f: (layer, 0, f)),
            pl.BlockSpec((None, d, FF_TILE_HEAD), lambda f: (layer, 0, f)),
            pl.BlockSpec((None, FF_TILE_HEAD, d), lambda f: (layer, f, 0)),
        ],
        out_specs=[
            pl.BlockSpec((bm, d), lambda f: (0, 0)),
            pl.BlockSpec((None, d, FF_TILE_HEAD), lambda f: (f, 0, 0)),
            pl.BlockSpec((None, d, FF_TILE_HEAD), lambda f: (f, 0, 0)),
            pl.BlockSpec((FF_TILE_HEAD, d), lambda f: (f, 0)),
        ],
        out_shape=[
            jax.ShapeDtypeStruct((rows, d), F32),
            jax.ShapeDtypeStruct((d_ff // FF_TILE_HEAD, d, FF_TILE_HEAD), BF16),
            jax.ShapeDtypeStruct((d_ff // FF_TILE_HEAD, d, FF_TILE_HEAD), BF16),
            jax.ShapeDtypeStruct((d_ff, d), BF16),
        ],
        scratch_shapes=[pltpu.VMEM((bm, d), BF16)],
        compiler_params=_params("arbitrary"),
        name="ffn_head",
    )(x, gain, wg, wu, wd)
    return pl.pallas_call(
        _ffn_tail_body,
        grid=(ROW_TILES - 1, d_ff // FF_TILE),
        in_specs=[
            pl.BlockSpec((bm, d), lambda i, f: (i + 1, 0)),
            gain_spec(2),
            pl.BlockSpec((FF_TILE // FF_TILE_HEAD, d, FF_TILE_HEAD), lambda i, f: (f, 0, 0)),
            pl.BlockSpec((FF_TILE // FF_TILE_HEAD, d, FF_TILE_HEAD), lambda i, f: (f, 0, 0)),
            pl.BlockSpec((FF_TILE, d), lambda i, f: (f, 0)),
            pl.BlockSpec(memory_space=pl.ANY),
        ],
        out_specs=pl.BlockSpec((bm, d), lambda i, f: (i + 1, 0)),
        out_shape=jax.ShapeDtypeStruct((rows, d), F32),
        scratch_shapes=[pltpu.VMEM((bm, d), BF16)],
        input_output_aliases={5: 0},
        compiler_params=_params("arbitrary", "arbitrary"),
        name="ffn_tail",
    )(x, gain, wgb, wub, wdb, out)


def _norm_body(x_ref, g_ref, o_ref):
    o_ref[...] = _rms(x_ref[...], g_ref[...])


def _final_norm(x, gain, rows_per_block, first_block, n_blocks):
    d = x.shape[-1]
    return pl.pallas_call(
        _norm_body,
        grid=(n_blocks,),
        in_specs=[
            pl.BlockSpec((rows_per_block, d), lambda i: (first_block + i, 0)),
            pl.BlockSpec((1, d), lambda i: (0, 0)),
        ],
        out_specs=pl.BlockSpec((rows_per_block, d), lambda i: (i, 0)),
        out_shape=jax.ShapeDtypeStruct((rows_per_block * n_blocks, d), F32),
        compiler_params=_params("arbitrary"),
        name="final_norm",
    )(x, gain)


def _proj_steps(j, n_in, x_ref, gm_ref, win_ref, wout_ref, o_ref, winb_ref, woutb_ref, xn_scr, p_scr, mg_scr, mix_fn):
    n_out = x_ref.shape[-1] // N_TILE

    @pl.when(j == 0)
    def _():
        xn_scr[...] = _rms(x_ref[...], gm_ref[...]).astype(BF16)

    @pl.when(j < n_in)
    def _():
        w = win_ref[...].astype(BF16)
        winb_ref[...] = w
        p_scr[j] = jnp.dot(xn_scr[...], w, preferred_element_type=F32)

    pl.when(j == n_in)(mix_fn)

    for n in range(n_out):
        @pl.when(j == n_in + 1 + n)
        def _():
            w = wout_ref[...].astype(BF16)
            woutb_ref[...] = w
            o_ref[...] = x_ref[:, n * N_TILE:(n + 1) * N_TILE] + jnp.dot(mg_scr[...], w, preferred_element_type=F32)


def _pcol(p_scr, col):
    return p_scr[col // N_TILE, :, col % N_TILE:col % N_TILE + HEAD_DIM]


def _lru_gates(xconv, wri, b_r, b_i, c_lam):
    ri = jnp.dot(xconv.astype(BF16), wri, preferred_element_type=F32)
    r = jax.nn.sigmoid(ri[:, :HEAD_DIM] + b_r)
    i = jax.nn.sigmoid(ri[:, HEAD_DIM:] + b_i)
    log_a = c_lam * r
    a = jnp.exp(log_a)
    mult = jnp.sqrt(-jnp.tanh(log_a) * (a * a + 1.0))
    return a, mult * (i * xconv)


def _mix_prompt_body(tiles, x_ref, xres_ref, o_in_ref, gm_ref, win_ref, wout_ref, vg_ref, ws_ref, bs_ref, cw_ref, cb_ref,
                     wri_ref, br_ref, bi_ref, lam_ref, on_ref,
                     o_ref, conv_ref, h_ref, v_ref,
                     p_scr, xn_scr, mg_scr, xc_scr, a_scr, b_scr, ao_scr, hc_scr):
    del o_in_ref
    rows = x_ref.shape[0]
    d_a = ws_ref.shape[0] * HEAD_DIM
    d_b = wri_ref.shape[0] * HEAD_DIM
    n_grp = rows // V7X_SUBLANES
    lead = V7X_SUBLANES
    s = pl.program_id(0)
    p_new, p_mix = s % 2, (s + 1) % 2
    m_mix, m_out = (s + 1) % 2, s % 2
    first_tile = (jnp.maximum(s - 1, 0) % tiles) == 0

    @pl.when(s == 0)
    def _():
        p_scr[1] = jnp.zeros(p_scr.shape[1:], F32)
        mg_scr[0] = jnp.zeros(mg_scr.shape[1:], BF16)
        xc_scr[0:lead, :] = jnp.zeros((lead, d_b), F32)
        hc_scr[...] = jnp.zeros_like(hc_scr)

    xn_scr[...] = _rms(x_ref[...], gm_ref[...]).astype(BF16)
    matmuls = []
    for k in range(wout_ref.shape[-1] // PROJ_CHUNK):
        def out_chunk(cs=pl.ds(k * PROJ_CHUNK, PROJ_CHUNK)):
            o_ref[:, cs] = xres_ref[:, cs] + jnp.dot(mg_scr[m_out], wout_ref[:, cs], preferred_element_type=F32)
        matmuls.append(out_chunk)
    for k in range(win_ref.shape[-1] // PROJ_CHUNK):
        def in_chunk(first=k * (PROJ_CHUNK // HEAD_DIM), cs=pl.ds(k * PROJ_CHUNK, PROJ_CHUNK)):
            res = jnp.dot(xn_scr[...], win_ref[:, cs], preferred_element_type=F32)
            for j in range(PROJ_CHUNK // HEAD_DIM):
                p_scr[p_new, first + j] = res[:, j * HEAD_DIM:(j + 1) * HEAD_DIM]
        matmuls.append(in_chunk)
    matmuls = iter(matmuls)

    def project(n_chunks):
        for _ in range(n_chunks):
            next(matmuls)()

    pcol = lambda col: p_scr[p_mix, col // HEAD_DIM]

    rr = lax.broadcasted_iota(jnp.int32, (CHUNK, CHUNK), 0)
    cc = lax.broadcasted_iota(jnp.int32, (CHUNK, CHUNK), 1)
    tril = (cc <= rr).astype(F32)
    for h in range(d_a // HEAD_DIM):
        project(MLP_HEAD_CHUNKS[h])
        cols = slice(h * HEAD_DIM, (h + 1) * HEAD_DIM)
        wsm = (ws_ref[h] * tril).astype(BF16)
        bcol = bs_ref[:, h:h + 1]
        for c in range(rows // CHUNK):
            rs = slice(c * CHUNK, (c + 1) * CHUNK)
            vh = _rms(p_scr[p_mix, (d_a + h * HEAD_DIM) // HEAD_DIM, rs, :], vg_ref[:, cols])
            if c == rows // CHUNK - 1:
                v_ref[:, cols] = vh
            z = jnp.dot(wsm, vh.astype(BF16), preferred_element_type=F32) + bcol
            ao_scr[rs, cols] = p_scr[p_mix, h, rs, :] * z
    mg_scr[m_mix, :, :d_a] = _rms(ao_scr[...], on_ref[:, :d_a]).astype(BF16)

    xc_scr[0:lead, :] = jnp.where(first_tile, 0.0, xc_scr[0:lead, :])
    for h in range(d_b // HEAD_DIM):
        cols = slice(h * HEAD_DIM, (h + 1) * HEAD_DIM)
        xc_scr[lead:lead + rows, cols] = pcol(2 * d_a + h * HEAD_DIM)
    conv_ref[...] = xc_scr[lead + rows - (CONV_W - 1):lead + rows, :]
    c_lam = -LRU_C * jax.nn.softplus(-lam_ref[...])
    blk = rows // LRU_ROW_SPLIT
    sub = lax.broadcasted_iota(jnp.int32, (blk // V7X_SUBLANES, V7X_SUBLANES, HEAD_DIM), 1)
    for h in range(d_b // HEAD_DIM):
        project(LRU_HEAD_CHUNKS[h])
        cols = slice(h * HEAD_DIM, (h + 1) * HEAD_DIM)
        for r0 in range(0, rows, blk):
            acc = cw_ref[0:1, cols] * xc_scr[lead - 3 + r0:lead - 3 + r0 + blk, cols]
            for k in range(1, CONV_W):
                acc = acc + cw_ref[k:k + 1, cols] * xc_scr[lead - 3 + k + r0:lead - 3 + k + r0 + blk, cols]
            xconv = cb_ref[:, cols] + acc
            a, bx = _lru_gates(xconv, wri_ref[h], br_ref[:, cols], bi_ref[:, cols], c_lam[:, cols])
            a3 = a.reshape(blk // V7X_SUBLANES, V7X_SUBLANES, HEAD_DIM)
            b3 = bx.reshape(blk // V7X_SUBLANES, V7X_SUBLANES, HEAD_DIM)
            for k in (1, 2, 4):
                a_prev = pltpu.roll(a3, k, axis=1)
                b_prev = pltpu.roll(b3, k, axis=1)
                keep = sub >= k
                b3 = jnp.where(keep, b3 + a3 * b_prev, b3)
                a3 = jnp.where(keep, a3 * a_prev, a3)
            a_scr[r0:r0 + blk, cols] = a3.reshape(blk, HEAD_DIM)
            b_scr[r0:r0 + blk, cols] = b3.reshape(blk, HEAD_DIM)
    xc_scr[0:lead, :] = xc_scr[rows:rows + lead, :]

    project(TAIL_CHUNKS[0])
    h_prev = jnp.where(first_tile, 0.0, hc_scr[...])
    for g in range(n_grp):
        gs = slice(g * V7X_SUBLANES, (g + 1) * V7X_SUBLANES)
        hs = b_scr[gs, :] + a_scr[gs, :] * h_prev
        b_scr[gs, :] = hs
        h_prev = jnp.broadcast_to(hs[V7X_SUBLANES - 1:, :], (V7X_SUBLANES, d_b))
    hc_scr[...] = h_prev
    h_ref[...] = h_prev[0:1, :]

    for h in range(d_b // HEAD_DIM):
        cols = slice(h * HEAD_DIM, (h + 1) * HEAD_DIM)
        ao_scr[:, cols] = b_scr[:, cols] * _gelu_tanh(pcol(2 * d_a + d_b + h * HEAD_DIM))
    mg_scr[m_mix, :, d_a:] = _rms(ao_scr[...], on_ref[:, d_a:]).astype(BF16)
    assert next(matmuls, None) is None


def _mix_sample_body(x_ref, gm_ref, win_ref, wout_ref, sc_ref, h0_ref, vg_ref, ws0_ref, bs0_ref, cw_ref,
                     cb_ref, wri_ref, br_ref, bi_ref, lam_ref, on_ref,
                     o_ref, conv_ref, h_ref, v_ref, winb_ref, woutb_ref,
                     xn_scr, p_scr, mg_scr, y_scr):
    d_a = vg_ref.shape[-1]
    d_b = wri_ref.shape[0] * HEAD_DIM

    def mix():
        for h in range(d_a // HEAD_DIM):
            cols = slice(h * HEAD_DIM, (h + 1) * HEAD_DIM)
            vh = _rms(_pcol(p_scr, d_a + h * HEAD_DIM), vg_ref[:, cols])
            v_ref[:, cols] = vh
            y_scr[:, cols] = _pcol(p_scr, h * HEAD_DIM) * (ws0_ref[:, cols] * vh + bs0_ref[:, cols])
        mg_scr[:, :d_a] = _rms(y_scr[...], on_ref[:, :d_a]).astype(BF16)

        for k in range(CONV_W - 2):
            conv_ref[k] = sc_ref[k + 1]
        c_lam = -LRU_C * jax.nn.softplus(-lam_ref[...])
        for h in range(d_b // HEAD_DIM):
            cols = slice(h * HEAD_DIM, (h + 1) * HEAD_DIM)
            xb = _pcol(p_scr, 2 * d_a + h * HEAD_DIM)
            conv_ref[CONV_W - 2, :, cols] = xb
            acc = cw_ref[0:1, cols] * sc_ref[0, :, cols]
            for k in range(1, CONV_W - 1):
                acc = acc + cw_ref[k:k + 1, cols] * sc_ref[k, :, cols]
            acc = acc + cw_ref[CONV_W - 1:CONV_W, cols] * xb
            xconv = cb_ref[:, cols] + acc
            a, bx = _lru_gates(xconv, wri_ref[h], br_ref[:, cols], bi_ref[:, cols], c_lam[:, cols])
            hs = a * h0_ref[:, cols] + bx
            h_ref[:, cols] = hs
            y_scr[:, cols] = hs * _gelu_tanh(_pcol(p_scr, 2 * d_a + d_b + h * HEAD_DIM))
        mg_scr[:, d_a:] = _rms(y_scr[...], on_ref[:, d_a:]).astype(BF16)

    _proj_steps(pl.program_id(0), p_scr.shape[0], x_ref, gm_ref, win_ref, wout_ref, o_ref, winb_ref, woutb_ref,
                xn_scr, p_scr, mg_scr, mix)


def _mixer_layer(x, batch, seq, layer, gm, w_in, w_out, sc_t, h0, vg, ws, bs_t, ws0, bs0, cw, cb, wri, br, bi, lam,
                 on):
    total_rows, d = x.shape
    d_in = w_in.shape[-1]
    n_a, n_b = ws.shape[1], wri.shape[1]
    d_a, d_b = n_a * HEAD_DIM, n_b * HEAD_DIM
    n_in, n_out = d_in // N_TILE, d // N_TILE
    steps = n_in + 1 + n_out
    tiles = seq // MIX_ROWS
    s_rows = total_rows - batch * seq
    s_blk = (batch * seq) // s_rows

    ix = lambda fn: (lambda *g: fn(g[-1]))
    ocol = lambda j: jnp.clip(j - n_in - 1, 0, n_out - 1)
    vec = lambda n: pl.BlockSpec((None, 1, n), ix(lambda j: (layer, 0, 0)))
    full = lambda *shape: pl.BlockSpec((None,) + shape, ix(lambda j: (layer,) + (0,) * len(shape)))
    win = pl.BlockSpec((None, d, N_TILE), ix(lambda j: (layer, 0, jnp.minimum(j, n_in - 1))))
    wout = pl.BlockSpec((None, d_a + d_b, N_TILE), ix(lambda j: (layer, 0, ocol(j))))

    x_new, conv_s, h_s, v_s, w_in_b, w_out_b = pl.pallas_call(
        _mix_sample_body,
        grid=(steps,),
        in_specs=[
            pl.BlockSpec((s_rows, d), lambda j: (s_blk, 0)),
            vec(d), win, wout,
            full(CONV_W - 1, s_rows, d_b), full(s_rows, d_b),
            vec(d_a), vec(d_a), vec(d_a), full(CONV_W, d_b), vec(d_b),
            full(n_b, HEAD_DIM, 2 * HEAD_DIM), vec(d_b), vec(d_b), vec(d_b), vec(d_a + d_b),
        ],
        out_specs=[
            pl.BlockSpec((s_rows, N_TILE), lambda j: (s_blk, ocol(j))),
            pl.BlockSpec((CONV_W - 1, s_rows, d_b), lambda j: (0, 0, 0)),
            pl.BlockSpec((s_rows, d_b), lambda j: (0, 0)),
            pl.BlockSpec((s_rows, d_a), lambda j: (0, 0)),
            pl.BlockSpec((d, N_TILE), lambda j: (0, jnp.minimum(j, n_in - 1))),
            pl.BlockSpec((d_a + d_b, N_TILE), lambda j: (0, ocol(j))),
        ],
        out_shape=[
            jax.ShapeDtypeStruct((total_rows, d), F32),
            jax.ShapeDtypeStruct((CONV_W - 1, s_rows, d_b), F32),
            jax.ShapeDtypeStruct((s_rows, d_b), F32),
            jax.ShapeDtypeStruct((s_rows, d_a), F32),
            jax.ShapeDtypeStruct((d, d_in), BF16),
            jax.ShapeDtypeStruct((d_a + d_b, d), BF16),
        ],
        scratch_shapes=[
            pltpu.VMEM((s_rows, d), BF16),
            pltpu.VMEM((n_in, s_rows, N_TILE), F32),
            pltpu.VMEM((s_rows, d_a + d_b), BF16),
            pltpu.VMEM((s_rows, d_a), F32),
        ],
        compiler_params=_params("arbitrary"),
        name="mix_sample",
    )(x, gm, w_in, w_out, sc_t, h0, vg, ws0, bs0, cw, cb, wri, br, bi, lam, on)

    n_tiles = batch * tiles
    mixed = lambda s: jnp.maximum(s - 1, 0) // tiles
    resident = lambda *shape: pl.BlockSpec(shape, lambda s: (0,) * len(shape))
    x_new, conv_p, h_p, v_p = pl.pallas_call(
        functools.partial(_mix_prompt_body, tiles),
        grid=(n_tiles + 2,),
        in_specs=[
            pl.BlockSpec((MIX_ROWS, d), lambda s: (jnp.minimum(s, n_tiles - 1), 0)),
            pl.BlockSpec((MIX_ROWS, d), lambda s: (jnp.maximum(s - 2, 0), 0)),
            pl.BlockSpec(memory_space=pl.ANY),
            vec(d), resident(d, d_in), resident(d_a + d_b, d), vec(d_a),
            full(n_a, CHUNK, CHUNK), full(CHUNK, n_a), full(CONV_W, d_b), vec(d_b),
            full(n_b, HEAD_DIM, 2 * HEAD_DIM), vec(d_b), vec(d_b), vec(d_b), vec(d_a + d_b),
        ],
        out_specs=[
            pl.BlockSpec((MIX_ROWS, d), lambda s: (jnp.maximum(s - 2, 0), 0)),
            pl.BlockSpec((None, CONV_W - 1, d_b), lambda s: (mixed(s), 0, 0)),
            pl.BlockSpec((None, 1, d_b), lambda s: (mixed(s), 0, 0)),
            pl.BlockSpec((None, CHUNK, d_a), lambda s: (mixed(s), 0, 0)),
        ],
        out_shape=[
            jax.ShapeDtypeStruct((total_rows, d), F32),
            jax.ShapeDtypeStruct((batch + 1, CONV_W - 1, d_b), F32),
            jax.ShapeDtypeStruct((batch + 1, 1, d_b), F32),
            jax.ShapeDtypeStruct((batch + 1, CHUNK, d_a), F32),
        ],
        scratch_shapes=[
            pltpu.VMEM((2, d_in // HEAD_DIM, MIX_ROWS, HEAD_DIM), F32),
            pltpu.VMEM((MIX_ROWS, d), BF16),
            pltpu.VMEM((2, MIX_ROWS, d_a + d_b), BF16),
            pltpu.VMEM((MIX_ROWS + 2 * V7X_SUBLANES, d_b), F32),
            pltpu.VMEM((MIX_ROWS, d_b), F32),
            pltpu.VMEM((MIX_ROWS, d_b), F32),
            pltpu.VMEM((MIX_ROWS, d_a), F32),
            pltpu.VMEM((V7X_SUBLANES, d_b), F32),
        ],
        input_output_aliases={2: 0},
        compiler_params=_params("arbitrary"),
        name="mix_prompt",
    )(x, x, x_new, gm, w_in_b, w_out_b, vg, ws, bs_t, cw, cb, wri, br, bi, lam, on)

    return x_new, conv_p[:batch], h_p[:batch], v_p[:batch], conv_s, h_s, v_s


def kernel(x_prompt, x_sample, state_conv, state_h, ffn1_norm, ffn1_wg, ffn1_wu, ffn1_wd, mix_norm, w_in, v_norm,
           w_spatial, b_spatial, conv_w, conv_b, w_rgate, b_rgate, w_igate, b_igate, lru_lambda, out_norm, w_out,
           ffn2_norm, ffn2_wg, ffn2_wu, ffn2_wd, final_norm):
    batch, seq, d_model = x_prompt.shape
    dec_batch, dec_seq, _ = x_sample.shape
    depth = w_in.shape[0]
    assert dec_seq == 1 and seq % MIX_ROWS == 0 and MIX_ROWS % CHUNK == 0 and (batch * seq) % NORM_ROWS == 0
    p_rows = batch * seq
    rows = p_rows + dec_batch
    assert rows % (ROW_TILES * 2 * V7X_SUBLANES) == 0 and p_rows % dec_batch == 0

    x = jnp.concatenate([x_prompt.reshape(p_rows, d_model), x_sample.reshape(dec_batch, d_model)], axis=0)

    row3 = lambda a: a.reshape(a.shape[0], 1, a.shape[-1])
    bf = lambda a: a.astype(BF16)
    g1, gm, g2 = row3(ffn1_norm), row3(mix_norm), row3(ffn2_norm)
    wri = bf(jnp.concatenate([w_rgate, w_igate], axis=-1))
    vg, cb, br, bi, lam, on = (row3(a) for a in (v_norm, conv_b, b_rgate, b_igate, lru_lambda, out_norm))
    bs_t = jnp.transpose(b_spatial, (0, 2, 1))
    ws0 = row3(jnp.repeat(w_spatial[:, :, 0, 0], HEAD_DIM, axis=-1))
    bs0 = row3(jnp.repeat(b_spatial[:, :, 0], HEAD_DIM, axis=-1))
    sc_t = jnp.transpose(state_conv, (0, 2, 1, 3))

    conv_p, h_p, v_p, conv_s, h_s, v_s = [], [], [], [], [], []
    for l in range(depth):
        x = _ffn(x, g1, ffn1_wg, ffn1_wu, ffn1_wd, l)
        x, cp, hp, vp, cs, hs, vs = _mixer_layer(x, batch, seq, l, gm, w_in, w_out, sc_t, state_h, vg,
                                                 w_spatial, bs_t, ws0, bs0, conv_w, cb, wri, br, bi, lam, on)
        x = _ffn(x, g2, ffn2_wg, ffn2_wu, ffn2_wd, l)
        conv_p.append(cp)
        h_p.append(hp.reshape(batch, -1))
        v_p.append(vp)
        conv_s.append(jnp.transpose(cs, (1, 0, 2)))
        h_s.append(hs)
        v_s.append(vs.reshape(dec_batch, dec_seq, -1))

    gf = final_norm.reshape(1, d_model)
    y_prompt = _final_norm(x, gf, NORM_ROWS, 0, p_rows // NORM_ROWS).reshape(batch, seq, d_model)
    y_sample = _final_norm(x, gf, dec_batch, p_rows // dec_batch, 1).reshape(dec_batch, dec_seq, d_model)
    return (y_prompt, y_sample, jnp.stack(conv_p), jnp.stack(h_p), jnp.stack(v_p),
            jnp.stack(conv_s), jnp.stack(h_s), jnp.stack(v_s))
```

```python
import functools

import jax
import jax.numpy as jnp
from jax import lax
from jax.experimental import pallas as pl
from jax.experimental.pallas import tpu as pltpu

F32 = jnp.float32
BF16 = jnp.bfloat16

HEAD_DIM = 128
CHUNK = 128
CONV_W = 4
LRU_C = 8.0
EPS = 1e-6

V7X_SUBLANES = 8
V7X_VMEM_LIMIT_BYTES = 56 * 1024 * 1024

ROW_TILES = 8
FF_TILE = 512
FF_TILE_HEAD = 256
N_TILE = 512
MIX_ROWS = 256
NORM_ROWS = 1024
PROJ_CHUNK = 256
MLP_HEAD_CHUNKS = (1, 1, 1, 1, 1, 1, 1, 1)
LRU_HEAD_CHUNKS = (2, 2, 2, 2, 2, 2, 2, 2)
LRU_ROW_SPLIT = 2


def _params(*sem):
    return pltpu.CompilerParams(dimension_semantics=sem, vmem_limit_bytes=V7X_VMEM_LIMIT_BYTES)


def _rms(x, g):
    r = lax.rsqrt(jnp.mean(x * x, axis=-1, keepdims=True) + EPS)
    return x * r * g


def _gelu_tanh(x):
    c = 0.7978845608028654
    return x * (0.5 * (1.0 + jnp.tanh(c * (x + 0.044715 * (x * x * x)))))


def _ffn_step(first, x_ref, g_ref, weights, o_ref, xn_ref):
    @pl.when(first)
    def _():
        x = x_ref[...]
        xn_ref[...] = _rms(x, g_ref[...]).astype(BF16)
        o_ref[...] = x

    wg, wu, wd = weights()
    xn = xn_ref[...]
    gate = jnp.dot(xn, wg, preferred_element_type=F32)
    up = jnp.dot(xn, wu, preferred_element_type=F32)
    h = (gate * jax.nn.sigmoid(gate) * up * 0.5).astype(BF16)
    o_ref[...] += jnp.dot(h, wd, preferred_element_type=F32)


def _ffn_head_body(x_ref, g_ref, wg_ref, wu_ref, wd_ref, o_ref, wgub_ref, wdb_ref, xn_ref):
    def weights():
        wg, wu, wd = (ref[...].astype(BF16) for ref in (wg_ref, wu_ref, wd_ref))
        wgub_ref[:, :FF_TILE_HEAD] = wg
        wgub_ref[:, FF_TILE_HEAD:] = wu
        wdb_ref[...] = wd
        return wg, wu, wd

    _ffn_step(pl.program_id(0) == 0, x_ref, g_ref, weights, o_ref, xn_ref)


def _ffn_tail_body(x_ref, g_ref, wgu_ref, wd_ref, o_in_ref, o_ref, xn_ref):
    del o_in_ref

    def weights():
        tiles = range(wgu_ref.shape[0])
        wg = jnp.concatenate([wgu_ref[k, :, :FF_TILE_HEAD] for k in tiles], axis=-1)
        wu = jnp.concatenate([wgu_ref[k, :, FF_TILE_HEAD:] for k in tiles], axis=-1)
        return wg, wu, wd_ref[...]

    _ffn_step(pl.program_id(1) == 0, x_ref, g_ref, weights, o_ref, xn_ref)


def _ffn(x, gain, wg, wu, wd, layer):
    rows, d = x.shape
    d_ff = wg.shape[-1]
    bm = rows // ROW_TILES
    gain_spec = lambda nd: pl.BlockSpec((None, 1, d), (lambda f: (layer, 0, 0)) if nd == 1
                                        else (lambda i, f: (layer, 0, 0)))
    out, wgub, wdb = pl.pallas_call(
        _ffn_head_body,
        grid=(d_ff // FF_TILE_HEAD,),
        in_specs=[
            pl.BlockSpec((bm, d), lambda f: (0, 0), pipeline_mode=pl.Buffered(1)),
            gain_spec(1),
            pl.BlockSpec((None, d, FF_TILE_HEAD), lambda f: (layer, 0, f)),
            pl.BlockSpec((None, d, FF_TILE_HEAD), lambda f: (layer, 0, f)),
            pl.BlockSpec((None, FF_TILE_HEAD, d), lambda f: (layer, f, 0)),
        ],
        out_specs=[
            pl.BlockSpec((bm, d), lambda f: (0, 0)),
            pl.BlockSpec((None, d, 2 * FF_TILE_HEAD), lambda f: (f, 0, 0)),
            pl.BlockSpec((FF_TILE_HEAD, d), lambda f: (f, 0)),
        ],
        out_shape=[
            jax.ShapeDtypeStruct((rows, d), F32),
            jax.ShapeDtypeStruct((d_ff // FF_TILE_HEAD, d, 2 * FF_TILE_HEAD), BF16),
            jax.ShapeDtypeStruct((d_ff, d), BF16),
        ],
        scratch_shapes=[pltpu.VMEM((bm, d), BF16)],
        compiler_params=_params("arbitrary"),
        name="ffn_head",
    )(x, gain, wg, wu, wd)
    return pl.pallas_call(
        _ffn_tail_body,
        grid=(ROW_TILES - 1, d_ff // FF_TILE),
        in_specs=[
            pl.BlockSpec((bm, d), lambda i, f: (i + 1, 0)),
            gain_spec(2),
            pl.BlockSpec((FF_TILE // FF_TILE_HEAD, d, 2 * FF_TILE_HEAD), lambda i, f: (f, 0, 0)),
            pl.BlockSpec((FF_TILE, d), lambda i, f: (f, 0)),
            pl.BlockSpec(memory_space=pl.ANY),
        ],
        out_specs=pl.BlockSpec((bm, d), lambda i, f: (i + 1, 0)),
        out_shape=jax.ShapeDtypeStruct((rows, d), F32),
        scratch_shapes=[pltpu.VMEM((bm, d), BF16)],
        input_output_aliases={4: 0},
        compiler_params=_params("arbitrary", "arbitrary"),
        name="ffn_tail",
    )(x, gain, wgub, wdb, out)


def _norm_body(x_ref, g_ref, o_ref):
    o_ref[...] = _rms(x_ref[...], g_ref[...])


def _final_norm(x, gain, rows_per_block, first_block, n_blocks):
    d = x.shape[-1]
    return pl.pallas_call(
        _norm_body,
        grid=(n_blocks,),
        in_specs=[
            pl.BlockSpec((rows_per_block, d), lambda i: (first_block + i, 0)),
            pl.BlockSpec((1, d), lambda i: (0, 0)),
        ],
        out_specs=pl.BlockSpec((rows_per_block, d), lambda i: (i, 0)),
        out_shape=jax.ShapeDtypeStruct((rows_per_block * n_blocks, d), F32),
        compiler_params=_params("arbitrary"),
        name="final_norm",
    )(x, gain)


def _proj_steps(j, n_in, x_ref, gm_ref, win_ref, wout_ref, o_ref, winb_ref, woutb_ref, xn_scr, p_scr, mg_scr, mix_fn):
    n_out = x_ref.shape[-1] // N_TILE

    @pl.when(j == 0)
    def _():
        xn_scr[...] = _rms(x_ref[...], gm_ref[...]).astype(BF16)

    @pl.when(j < n_in)
    def _():
        w = win_ref[...].astype(BF16)
        winb_ref[...] = w
        p_scr[j] = jnp.dot(xn_scr[...], w, preferred_element_type=F32)

    pl.when(j == n_in)(mix_fn)

    for n in range(n_out):
        @pl.when(j == n_in + 1 + n)
        def _():
            w = wout_ref[...].astype(BF16)
            woutb_ref[...] = w
            o_ref[...] = x_ref[:, n * N_TILE:(n + 1) * N_TILE] + jnp.dot(mg_scr[...], w, preferred_element_type=F32)


def _pcol(p_scr, col):
    return p_scr[col // N_TILE, :, col % N_TILE:col % N_TILE + HEAD_DIM]


def _lru_gates(xconv, wri, b_r, b_i, c_lam):
    ri = jnp.dot(xconv.astype(BF16), wri, preferred_element_type=F32)
    r = jax.nn.sigmoid(ri[:, :HEAD_DIM] + b_r)
    i = jax.nn.sigmoid(ri[:, HEAD_DIM:] + b_i)
    log_a = c_lam * r
    a = jnp.exp(log_a)
    mult = jnp.sqrt(-jnp.tanh(log_a) * (a * a + 1.0))
    return a, mult * (i * xconv)


def _mix_prompt_body(tiles, x_ref, xres_ref, o_in_ref, gm_ref, win_ref, wout_ref, vg_ref, ws_ref, bs_ref, cw_ref,
                     cb_ref, wri_ref, br_ref, bi_ref, lam_ref, on_ref,
                     o_ref, conv_ref, h_ref, v_ref,
                     p_scr, xn_scr, mg_scr, xc_scr, a_scr, b_scr, ao_scr, hc_scr):
    del o_in_ref
    rows = x_ref.shape[0]
    d_a = ws_ref.shape[0] * HEAD_DIM
    d_b = wri_ref.shape[0] * HEAD_DIM
    n_grp = rows // V7X_SUBLANES
    lead = V7X_SUBLANES
    s = pl.program_id(0)
    p_new, p_mix = s % 2, (s + 1) % 2
    m_mix, m_out = (s + 1) % 2, s % 2
    first_tile = (jnp.maximum(s - 1, 0) % tiles) == 0

    @pl.when(s == 0)
    def _():
        p_scr[1] = jnp.zeros(p_scr.shape[1:], F32)
        mg_scr[0] = jnp.zeros(mg_scr.shape[1:], BF16)
        xc_scr[0:lead, :] = jnp.zeros((lead, d_b), F32)
        hc_scr[...] = jnp.zeros_like(hc_scr)

    xn_scr[...] = _rms(x_ref[...], gm_ref[...]).astype(BF16)
    matmuls = []
    for k in range(wout_ref.shape[-1] // PROJ_CHUNK):
        def out_chunk(cs=pl.ds(k * PROJ_CHUNK, PROJ_CHUNK)):
            o_ref[:, cs] = xres_ref[:, cs] + jnp.dot(mg_scr[m_out], wout_ref[:, cs], preferred_element_type=F32)
        matmuls.append(out_chunk)
    for k in range(win_ref.shape[-1] // PROJ_CHUNK):
        def in_chunk(first=k * (PROJ_CHUNK // HEAD_DIM), cs=pl.ds(k * PROJ_CHUNK, PROJ_CHUNK)):
            res = jnp.dot(xn_scr[...], win_ref[:, cs], preferred_element_type=F32)
            for j in range(PROJ_CHUNK // HEAD_DIM):
                p_scr[p_new, first + j] = res[:, j * HEAD_DIM:(j + 1) * HEAD_DIM]
        matmuls.append(in_chunk)
    matmuls = iter(matmuls)

    def project(n_chunks):
        for _ in range(n_chunks):
            next(matmuls)()

    pcol = lambda col: p_scr[p_mix, col // HEAD_DIM]

    rr = lax.broadcasted_iota(jnp.int32, (CHUNK, CHUNK), 0)
    cc = lax.broadcasted_iota(jnp.int32, (CHUNK, CHUNK), 1)
    tril = (cc <= rr).astype(F32)
    for h in range(d_a // HEAD_DIM):
        project(MLP_HEAD_CHUNKS[h])
        cols = slice(h * HEAD_DIM, (h + 1) * HEAD_DIM)
        wsm = (ws_ref[h] * tril).astype(BF16)
        bcol = bs_ref[:, h:h + 1]
        for c in range(rows // CHUNK):
            rs = slice(c * CHUNK, (c + 1) * CHUNK)
            vh = _rms(p_scr[p_mix, (d_a + h * HEAD_DIM) // HEAD_DIM, rs, :], vg_ref[:, cols])
            if c == rows // CHUNK - 1:
                v_ref[:, cols] = vh
            z = jnp.dot(wsm, vh.astype(BF16), preferred_element_type=F32) + bcol
            ao_scr[rs, cols] = p_scr[p_mix, h, rs, :] * z
    mg_scr[m_mix, :, :d_a] = _rms(ao_scr[...], on_ref[:, :d_a]).astype(BF16)

    xc_scr[0:lead, :] = jnp.where(first_tile, 0.0, xc_scr[0:lead, :])
    for h in range(d_b // HEAD_DIM):
        cols = slice(h * HEAD_DIM, (h + 1) * HEAD_DIM)
        xc_scr[lead:lead + rows, cols] = pcol(2 * d_a + h * HEAD_DIM)
    conv_ref[...] = xc_scr[lead + rows - (CONV_W - 1):lead + rows, :]
    c_lam = -LRU_C * jax.nn.softplus(-lam_ref[...])
    blk = rows // LRU_ROW_SPLIT
    sub = lax.broadcasted_iota(jnp.int32, (blk // V7X_SUBLANES, V7X_SUBLANES, HEAD_DIM), 1)
    for h in range(d_b // HEAD_DIM):
        project(LRU_HEAD_CHUNKS[h])
        cols = slice(h * HEAD_DIM, (h + 1) * HEAD_DIM)
        for r0 in range(0, rows, blk):
            acc = cw_ref[0:1, cols] * xc_scr[lead - 3 + r0:lead - 3 + r0 + blk, cols]
            for k in range(1, CONV_W):
                acc = acc + cw_ref[k:k + 1, cols] * xc_scr[lead - 3 + k + r0:lead - 3 + k + r0 + blk, cols]
            xconv = cb_ref[:, cols] + acc
            a, bx = _lru_gates(xconv, wri_ref[h], br_ref[:, cols], bi_ref[:, cols], c_lam[:, cols])
            a3 = a.reshape(blk // V7X_SUBLANES, V7X_SUBLANES, HEAD_DIM)
            b3 = bx.reshape(blk // V7X_SUBLANES, V7X_SUBLANES, HEAD_DIM)
            for k in (1, 2, 4):
                a_prev = pltpu.roll(a3, k, axis=1)
                b_prev = pltpu.roll(b3, k, axis=1)
                keep = sub >= k
                b3 = jnp.where(keep, b3 + a3 * b_prev, b3)
                a3 = jnp.where(keep, a3 * a_prev, a3)
            a_scr[r0:r0 + blk, cols] = a3.reshape(blk, HEAD_DIM)
            b_scr[r0:r0 + blk, cols] = b3.reshape(blk, HEAD_DIM)
    xc_scr[0:lead, :] = xc_scr[rows:rows + lead, :]

    h_prev = jnp.where(first_tile, 0.0, hc_scr[...])
    for g in range(n_grp):
        gs = slice(g * V7X_SUBLANES, (g + 1) * V7X_SUBLANES)
        hs = b_scr[gs, :] + a_scr[gs, :] * h_prev
        b_scr[gs, :] = hs
        h_prev = jnp.broadcast_to(hs[V7X_SUBLANES - 1:, :], (V7X_SUBLANES, d_b))
    hc_scr[...] = h_prev
    h_ref[...] = h_prev[0:1, :]

    for h in range(d_b // HEAD_DIM):
        cols = slice(h * HEAD_DIM, (h + 1) * HEAD_DIM)
        ao_scr[:, cols] = b_scr[:, cols] * _gelu_tanh(pcol(2 * d_a + d_b + h * HEAD_DIM))
    mg_scr[m_mix, :, d_a:] = _rms(ao_scr[...], on_ref[:, d_a:]).astype(BF16)
    assert next(matmuls, None) is None


def _mix_sample_body(x_ref, gm_ref, win_ref, wout_ref, sc_ref, h0_ref, vg_ref, ws0_ref, bs0_ref, cw_ref,
                     cb_ref, wri_ref, br_ref, bi_ref, lam_ref, on_ref,
                     o_ref, conv_ref, h_ref, v_ref, winb_ref, woutb_ref,
                     xn_scr, p_scr, mg_scr, y_scr):
    d_a = vg_ref.shape[-1]
    d_b = wri_ref.shape[0] * HEAD_DIM

    def mix():
        for h in range(d_a // HEAD_DIM):
            cols = slice(h * HEAD_DIM, (h + 1) * HEAD_DIM)
            vh = _rms(_pcol(p_scr, d_a + h * HEAD_DIM), vg_ref[:, cols])
            v_ref[:, cols] = vh
            y_scr[:, cols] = _pcol(p_scr, h * HEAD_DIM) * (ws0_ref[:, cols] * vh + bs0_ref[:, cols])
        mg_scr[:, :d_a] = _rms(y_scr[...], on_ref[:, :d_a]).astype(BF16)

        for k in range(CONV_W - 2):
            conv_ref[k] = sc_ref[k + 1]
        c_lam = -LRU_C * jax.nn.softplus(-lam_ref[...])
        for h in range(d_b // HEAD_DIM):
            cols = slice(h * HEAD_DIM, (h + 1) * HEAD_DIM)
            xb = _pcol(p_scr, 2 * d_a + h * HEAD_DIM)
            conv_ref[CONV_W - 2, :, cols] = xb
            acc = cw_ref[0:1, cols] * sc_ref[0, :, cols]
            for k in range(1, CONV_W - 1):
                acc = acc + cw_ref[k:k + 1, cols] * sc_ref[k, :, cols]
            acc = acc + cw_ref[CONV_W - 1:CONV_W, cols] * xb
            xconv = cb_ref[:, cols] + acc
            a, bx = _lru_gates(xconv, wri_ref[h], br_ref[:, cols], bi_ref[:, cols], c_lam[:, cols])
            hs = a * h0_ref[:, cols] + bx
            h_ref[:, cols] = hs
            y_scr[:, cols] = hs * _gelu_tanh(_pcol(p_scr, 2 * d_a + d_b + h * HEAD_DIM))
        mg_scr[:, d_a:] = _rms(y_scr[...], on_ref[:, d_a:]).astype(BF16)

    _proj_steps(pl.program_id(0), p_scr.shape[0], x_ref, gm_ref, win_ref, wout_ref, o_ref, winb_ref, woutb_ref,
                xn_scr, p_scr, mg_scr, mix)


def _mixer_layer(x, batch, seq, layer, gm, w_in, w_out, sc_t, h0, vg, ws, bs_t, ws0, bs0, cw, cb, wri, br, bi, lam,
                 on):
    total_rows, d = x.shape
    d_in = w_in.shape[-1]
    n_a, n_b = ws.shape[1], wri.shape[1]
    d_a, d_b = n_a * HEAD_DIM, n_b * HEAD_DIM
    n_in, n_out = d_in // N_TILE, d // N_TILE
    steps = n_in + 1 + n_out
    tiles = seq // MIX_ROWS
    s_rows = total_rows - batch * seq
    s_blk = (batch * seq) // s_rows

    ix = lambda fn: (lambda *g: fn(g[-1]))
    ocol = lambda j: jnp.clip(j - n_in - 1, 0, n_out - 1)
    vec = lambda n: pl.BlockSpec((None, 1, n), ix(lambda j: (layer, 0, 0)))
    full = lambda *shape: pl.BlockSpec((None,) + shape, ix(lambda j: (layer,) + (0,) * len(shape)))
    win = pl.BlockSpec((None, d, N_TILE), ix(lambda j: (layer, 0, jnp.minimum(j, n_in - 1))))
    wout = pl.BlockSpec((None, d_a + d_b, N_TILE), ix(lambda j: (layer, 0, ocol(j))))

    x_new, conv_s, h_s, v_s, w_in_b, w_out_b = pl.pallas_call(
        _mix_sample_body,
        grid=(steps,),
        in_specs=[
            pl.BlockSpec((s_rows, d), lambda j: (s_blk, 0)),
            vec(d), win, wout,
            full(CONV_W - 1, s_rows, d_b), full(s_rows, d_b),
            vec(d_a), vec(d_a), vec(d_a), full(CONV_W, d_b), vec(d_b),
            full(n_b, HEAD_DIM, 2 * HEAD_DIM), vec(d_b), vec(d_b), vec(d_b), vec(d_a + d_b),
        ],
        out_specs=[
            pl.BlockSpec((s_rows, N_TILE), lambda j: (s_blk, ocol(j))),
            pl.BlockSpec((CONV_W - 1, s_rows, d_b), lambda j: (0, 0, 0)),
            pl.BlockSpec((s_rows, d_b), lambda j: (0, 0)),
            pl.BlockSpec((s_rows, d_a), lambda j: (0, 0)),
            pl.BlockSpec((d, N_TILE), lambda j: (0, jnp.minimum(j, n_in - 1))),
            pl.BlockSpec((d_a + d_b, N_TILE), lambda j: (0, ocol(j))),
        ],
        out_shape=[
            jax.ShapeDtypeStruct((total_rows, d), F32),
            jax.ShapeDtypeStruct((CONV_W - 1, s_rows, d_b), F32),
            jax.ShapeDtypeStruct((s_rows, d_b), F32),
            jax.ShapeDtypeStruct((s_rows, d_a), F32),
            jax.ShapeDtypeStruct((d, d_in), BF16),
            jax.ShapeDtypeStruct((d_a + d_b, d), BF16),
        ],
        scratch_shapes=[
            pltpu.VMEM((s_rows, d), BF16),
            pltpu.VMEM((n_in, s_rows, N_TILE), F32),
            pltpu.VMEM((s_rows, d_a + d_b), BF16),
            pltpu.VMEM((s_rows, d_a), F32),
        ],
        compiler_params=_params("arbitrary"),
        name="mix_sample",
    )(x, gm, w_in, w_out, sc_t, h0, vg, ws0, bs0, cw, cb, wri, br, bi, lam, on)

    n_tiles = batch * tiles
    mixed = lambda s: jnp.maximum(s - 1, 0) // tiles
    resident = lambda *shape: pl.BlockSpec(shape, lambda s: (0,) * len(shape))
    x_new, conv_p, h_p, v_p = pl.pallas_call(
        functools.partial(_mix_prompt_body, tiles),
        grid=(n_tiles + 2,),
        in_specs=[
            pl.BlockSpec((MIX_ROWS, d), lambda s: (jnp.minimum(s, n_tiles - 1), 0)),
            pl.BlockSpec((MIX_ROWS, d), lambda s: (jnp.maximum(s - 2, 0), 0)),
            pl.BlockSpec(memory_space=pl.ANY),
            vec(d), resident(d, d_in), resident(d_a + d_b, d), vec(d_a),
            full(n_a, CHUNK, CHUNK), full(CHUNK, n_a), full(CONV_W, d_b), vec(d_b),
            full(n_b, HEAD_DIM, 2 * HEAD_DIM), vec(d_b), vec(d_b), vec(d_b), vec(d_a + d_b),
        ],
        out_specs=[
            pl.BlockSpec((MIX_ROWS, d), lambda s: (jnp.maximum(s - 2, 0), 0)),
            pl.BlockSpec((None, CONV_W - 1, d_b), lambda s: (mixed(s), 0, 0)),
            pl.BlockSpec((None, 1, d_b), lambda s: (mixed(s), 0, 0)),
            pl.BlockSpec((None, CHUNK, d_a), lambda s: (mixed(s), 0, 0)),
        ],
        out_shape=[
            jax.ShapeDtypeStruct((total_rows, d), F32),
            jax.ShapeDtypeStruct((batch + 1, CONV_W - 1, d_b), F32),
            jax.ShapeDtypeStruct((batch + 1, 1, d_b), F32),
            jax.ShapeDtypeStruct((batch + 1, CHUNK, d_a), F32),
        ],
        scratch_shapes=[
            pltpu.VMEM((2, d_in // HEAD_DIM, MIX_ROWS, HEAD_DIM), F32),
            pltpu.VMEM((MIX_ROWS, d), BF16),
            pltpu.VMEM((2, MIX_ROWS, d_a + d_b), BF16),
            pltpu.VMEM((MIX_ROWS + 2 * V7X_SUBLANES, d_b), F32),
            pltpu.VMEM((MIX_ROWS, d_b), F32),
            pltpu.VMEM((MIX_ROWS, d_b), F32),
            pltpu.VMEM((MIX_ROWS, d_a), F32),
            pltpu.VMEM((V7X_SUBLANES, d_b), F32),
        ],
        input_output_aliases={2: 0},
        compiler_params=_params("arbitrary"),
        name="mix_prompt",
    )(x, x, x_new, gm, w_in_b, w_out_b, vg, ws, bs_t, cw, cb, wri, br, bi, lam, on)

    return x_new, conv_p[:batch], h_p[:batch], v_p[:batch], conv_s, h_s, v_s


def kernel(x_prompt, x_sample, state_conv, state_h, ffn1_norm, ffn1_wg, ffn1_wu, ffn1_wd, mix_norm, w_in, v_norm,
           w_spatial, b_spatial, conv_w, conv_b, w_rgate, b_rgate, w_igate, b_igate, lru_lambda, out_norm, w_out,
           ffn2_norm, ffn2_wg, ffn2_wu, ffn2_wd, final_norm):
    batch, seq, d_model = x_prompt.shape
    dec_batch, dec_seq, _ = x_sample.shape
    depth = w_in.shape[0]
    assert dec_seq == 1 and seq % MIX_ROWS == 0 and MIX_ROWS % CHUNK == 0 and (batch * seq) % NORM_ROWS == 0
    p_rows = batch * seq
    rows = p_rows + dec_batch
    assert rows % (ROW_TILES * 2 * V7X_SUBLANES) == 0 and p_rows % dec_batch == 0

    x = jnp.concatenate([x_prompt.reshape(p_rows, d_model), x_sample.reshape(dec_batch, d_model)], axis=0)

    row3 = lambda a: a.reshape(a.shape[0], 1, a.shape[-1])
    bf = lambda a: a.astype(BF16)
    g1, gm, g2 = row3(ffn1_norm), row3(mix_norm), row3(ffn2_norm)
    wri = bf(jnp.concatenate([w_rgate, w_igate], axis=-1))
    vg, cb, br, bi, lam, on = (row3(a) for a in (v_norm, conv_b, b_rgate, b_igate, lru_lambda, out_norm))
    bs_t = jnp.transpose(b_spatial, (0, 2, 1))
    ws0 = row3(jnp.repeat(w_spatial[:, :, 0, 0], HEAD_DIM, axis=-1))
    bs0 = row3(jnp.repeat(b_spatial[:, :, 0], HEAD_DIM, axis=-1))
    sc_t = jnp.transpose(state_conv, (0, 2, 1, 3))

    conv_p, h_p, v_p, conv_s, h_s, v_s = [], [], [], [], [], []
    for l in range(depth):
        x = _ffn(x, g1, ffn1_wg, ffn1_wu, ffn1_wd, l)
        x, cp, hp, vp, cs, hs, vs = _mixer_layer(x, batch, seq, l, gm, w_in, w_out, sc_t, state_h, vg,
                                                 w_spatial, bs_t, ws0, bs0, conv_w, cb, wri, br, bi, lam, on)
        x = _ffn(x, g2, ffn2_wg, ffn2_wu, ffn2_wd, l)
        conv_p.append(cp)
        h_p.append(hp.reshape(batch, -1))
        v_p.append(vp)
        conv_s.append(jnp.transpose(cs, (1, 0, 2)))
        h_s.append(hs)
        v_s.append(vs.reshape(dec_batch, dec_seq, -1))

    gf = final_norm.reshape(1, d_model)
    y_prompt = _final_norm(x, gf, NORM_ROWS, 0, p_rows // NORM_ROWS).reshape(batch, seq, d_model)
    y_sample = _final_norm(x, gf, dec_batch, p_rows // dec_batch, 1).reshape(dec_batch, dec_seq, d_model)
    return (y_prompt, y_sample, jnp.stack(conv_p), jnp.stack(h_p), jnp.stack(v_p),
            jnp.stack(conv_s), jnp.stack(h_s), jnp.stack(v_s))
```

```python
import functools

import jax
import jax.numpy as jnp
from jax import lax
from jax.experimental import pallas as pl
from jax.experimental.pallas import tpu as pltpu

F32 = jnp.float32
BF16 = jnp.bfloat16

HEAD_DIM = 128
CHUNK = 128
CONV_W = 4
LRU_C = 8.0
EPS = 1e-6

V7X_SUBLANES = 8
V7X_VMEM_LIMIT_BYTES = 56 * 1024 * 1024

ROW_TILES = 8
FF_TILE = 512
FF_TILE_HEAD = 256
NORM_SLAB_ROWS = 128
N_TILE = 512
MIX_ROWS = 256
NORM_ROWS = 1024
PROJ_CHUNK = 256
MLP_HEAD_CHUNKS = (1, 1, 1, 1, 1, 1, 1, 1)
LRU_HEAD_CHUNKS = (2, 2, 2, 2, 2, 2, 2, 2)
LRU_ROW_SPLIT = 2


def _params(*sem):
    return pltpu.CompilerParams(dimension_semantics=sem, vmem_limit_bytes=V7X_VMEM_LIMIT_BYTES)


def _rms(x, g):
    r = lax.rsqrt(jnp.mean(x * x, axis=-1, keepdims=True) + EPS)
    return x * r * g


def _gelu_tanh(x):
    c = 0.7978845608028654
    return x * (0.5 * (1.0 + jnp.tanh(c * (x + 0.044715 * (x * x * x)))))


def _ffn_step(first, x_ref, g_ref, weights, o_ref, xn_ref):
    @pl.when(first)
    def _():
        x = x_ref[...]
        xn_ref[...] = _rms(x, g_ref[...]).astype(BF16)
        o_ref[...] = x

    wg, wu, wd = weights()
    xn = xn_ref[...]
    gate = jnp.dot(xn, wg, preferred_element_type=F32)
    up = jnp.dot(xn, wu, preferred_element_type=F32)
    h = (gate * jax.nn.sigmoid(gate) * up * 0.5).astype(BF16)
    o_ref[...] += jnp.dot(h, wd, preferred_element_type=F32)


def _ffn_head_body(x_ref, g_ref, wg_ref, wu_ref, wd_ref, o_ref, wgub_ref, wdb_ref, xn_ref):
    def weights():
        wg, wu, wd = (ref[...].astype(BF16) for ref in (wg_ref, wu_ref, wd_ref))
        wgub_ref[:, :FF_TILE_HEAD] = wg
        wgub_ref[:, FF_TILE_HEAD:] = wu
        wdb_ref[...] = wd
        return wg, wu, wd

    _ffn_step(pl.program_id(0) == 0, x_ref, g_ref, weights, o_ref, xn_ref)


def _ffn_tail_body(x_hbm, g_ref, wgu_ref, wd_ref, o_in_ref, o_ref, xbuf, xn_ref, sem):
    del o_in_ref
    i, f = pl.program_id(0), pl.program_id(1)
    n_tiles = pl.num_programs(0)
    cur, nxt = i % 2, (i + 1) % 2
    bm = o_ref.shape[0]

    def x_copy(tile):
        return pltpu.make_async_copy(x_hbm.at[pl.ds((tile + 1) * bm, bm), :], xbuf.at[pl.ds(0, bm), :], sem.at[0])

    @pl.when((i == 0) & (f == 0))
    def _():
        xbuf[bm:, :] = jnp.zeros((xbuf.shape[0] - bm, xbuf.shape[1]), F32)
        x_copy(0).start()
        x_copy(0).wait()
        xn_ref[0] = _rms(xbuf[0:bm, :], g_ref[...]).astype(BF16)

    @pl.when(f == 0)
    def _():
        o_ref[...] = xbuf[0:bm, :]

    @pl.when((f == 0) & (i + 1 < n_tiles))
    def _():
        x_copy(i + 1).start()

    @pl.when((f == 1) & (i + 1 < n_tiles))
    def _():
        x_copy(i + 1).wait()

    tiles = range(wgu_ref.shape[0])
    wg = jnp.concatenate([wgu_ref[k, :, :FF_TILE_HEAD] for k in tiles], axis=-1)
    wu = jnp.concatenate([wgu_ref[k, :, FF_TILE_HEAD:] for k in tiles], axis=-1)
    xn = xn_ref[cur]
    gate = jnp.dot(xn, wg, preferred_element_type=F32)
    up = jnp.dot(xn, wu, preferred_element_type=F32)
    h = (gate * jax.nn.sigmoid(gate) * up * 0.5).astype(BF16)
    o_ref[...] += jnp.dot(h, wd_ref[...], preferred_element_type=F32)

    slab_rows = NORM_SLAB_ROWS
    dst = pl.multiple_of(jnp.minimum(jnp.maximum(f - 2, 0) * slab_rows, bm - slab_rows), 2 * V7X_SUBLANES)
    src = pl.multiple_of(jnp.where(f < 2, bm, dst), 2 * V7X_SUBLANES)
    xn_ref[nxt, pl.ds(dst, slab_rows), :] = _rms(xbuf[pl.ds(src, slab_rows), :], g_ref[...]).astype(BF16)


def _ffn(x, gain, wg, wu, wd, layer):
    rows, d = x.shape
    d_ff = wg.shape[-1]
    bm = rows // ROW_TILES
    gain_spec = lambda nd: pl.BlockSpec((None, 1, d), (lambda f: (layer, 0, 0)) if nd == 1
                                        else (lambda i, f: (layer, 0, 0)))
    out, wgub, wdb = pl.pallas_call(
        _ffn_head_body,
        grid=(d_ff // FF_TILE_HEAD,),
        in_specs=[
            pl.BlockSpec((bm, d), lambda f: (0, 0), pipeline_mode=pl.Buffered(1)),
            gain_spec(1),
            pl.BlockSpec((None, d, FF_TILE_HEAD), lambda f: (layer, 0, f)),
            pl.BlockSpec((None, d, FF_TILE_HEAD), lambda f: (layer, 0, f)),
            pl.BlockSpec((None, FF_TILE_HEAD, d), lambda f: (layer, f, 0)),
        ],
        out_specs=[
            pl.BlockSpec((bm, d), lambda f: (0, 0)),
            pl.BlockSpec((None, d, 2 * FF_TILE_HEAD), lambda f: (f, 0, 0)),
            pl.BlockSpec((FF_TILE_HEAD, d), lambda f: (f, 0)),
        ],
        out_shape=[
            jax.ShapeDtypeStruct((rows, d), F32),
            jax.ShapeDtypeStruct((d_ff // FF_TILE_HEAD, d, 2 * FF_TILE_HEAD), BF16),
            jax.ShapeDtypeStruct((d_ff, d), BF16),
        ],
        scratch_shapes=[pltpu.VMEM((bm, d), BF16)],
        compiler_params=_params("arbitrary"),
        name="ffn_head",
    )(x, gain, wg, wu, wd)
    return pl.pallas_call(
        _ffn_tail_body,
        grid=(ROW_TILES - 1, d_ff // FF_TILE),
        in_specs=[
            pl.BlockSpec(memory_space=pl.ANY),
            gain_spec(2),
            pl.BlockSpec((FF_TILE // FF_TILE_HEAD, d, 2 * FF_TILE_HEAD), lambda i, f: (f, 0, 0)),
            pl.BlockSpec((FF_TILE, d), lambda i, f: (f, 0)),
            pl.BlockSpec(memory_space=pl.ANY),
        ],
        out_specs=pl.BlockSpec((bm, d), lambda i, f: (i + 1, 0)),
        out_shape=jax.ShapeDtypeStruct((rows, d), F32),
        scratch_shapes=[pltpu.VMEM((bm + NORM_SLAB_ROWS, d), F32), pltpu.VMEM((2, bm, d), BF16),
                        pltpu.SemaphoreType.DMA((1,))],
        input_output_aliases={4: 0},
        compiler_params=_params("arbitrary", "arbitrary"),
        name="ffn_tail",
    )(x, gain, wgub, wdb, out)


def _norm_body(x_ref, g_ref, o_ref):
    o_ref[...] = _rms(x_ref[...], g_ref[...])


def _final_norm(x, gain, rows_per_block, first_block, n_blocks):
    d = x.shape[-1]
    return pl.pallas_call(
        _norm_body,
        grid=(n_blocks,),
        in_specs=[
            pl.BlockSpec((rows_per_block, d), lambda i: (first_block + i, 0)),
            pl.BlockSpec((1, d), lambda i: (0, 0)),
        ],
        out_specs=pl.BlockSpec((rows_per_block, d), lambda i: (i, 0)),
        out_shape=jax.ShapeDtypeStruct((rows_per_block * n_blocks, d), F32),
        compiler_params=_params("arbitrary"),
        name="final_norm",
    )(x, gain)


def _proj_steps(j, n_in, x_ref, gm_ref, win_ref, wout_ref, o_ref, winb_ref, woutb_ref, xn_scr, p_scr, mg_scr, mix_fn):
    n_out = x_ref.shape[-1] // N_TILE

    @pl.when(j == 0)
    def _():
        xn_scr[...] = _rms(x_ref[...], gm_ref[...]).astype(BF16)

    @pl.when(j < n_in)
    def _():
        w = win_ref[...].astype(BF16)
        winb_ref[...] = w
        p_scr[j] = jnp.dot(xn_scr[...], w, preferred_element_type=F32)

    pl.when(j == n_in)(mix_fn)

    for n in range(n_out):
        @pl.when(j == n_in + 1 + n)
        def _():
            w = wout_ref[...].astype(BF16)
            woutb_ref[...] = w
            o_ref[...] = x_ref[:, n * N_TILE:(n + 1) * N_TILE] + jnp.dot(mg_scr[...], w, preferred_element_type=F32)


def _pcol(p_scr, col):
    return p_scr[col // N_TILE, :, col % N_TILE:col % N_TILE + HEAD_DIM]


def _lru_gates(xconv, wri, b_r, b_i, c_lam):
    ri = jnp.dot(xconv.astype(BF16), wri, preferred_element_type=F32)
    r = jax.nn.sigmoid(ri[:, :HEAD_DIM] + b_r)
    i = jax.nn.sigmoid(ri[:, HEAD_DIM:] + b_i)
    log_a = c_lam * r
    a = jnp.exp(log_a)
    mult = jnp.sqrt(-jnp.tanh(log_a) * (a * a + 1.0))
    return a, mult * (i * xconv)


def _mix_prompt_body(tiles, x_ref, xres_ref, o_in_ref, gm_ref, win_ref, wout_ref, vg_ref, ws_ref, bs_ref, cw_ref,
                     cb_ref, wri_ref, br_ref, bi_ref, lam_ref, on_ref,
                     o_ref, conv_ref, h_ref, v_ref,
                     p_scr, xn_scr, mg_scr, xc_scr, a_scr, b_scr, ao_scr, hc_scr):
    del o_in_ref
    rows = x_ref.shape[0]
    d_a = ws_ref.shape[0] * HEAD_DIM
    d_b = wri_ref.shape[0] * HEAD_DIM
    n_grp = rows // V7X_SUBLANES
    lead = V7X_SUBLANES
    s = pl.program_id(0)
    p_new, p_mix = s % 2, (s + 1) % 2
    m_mix, m_out = (s + 1) % 2, s % 2
    first_tile = (jnp.maximum(s - 1, 0) % tiles) == 0

    @pl.when(s == 0)
    def _():
        p_scr[1] = jnp.zeros(p_scr.shape[1:], F32)
        mg_scr[0] = jnp.zeros(mg_scr.shape[1:], BF16)
        xc_scr[0:lead, :] = jnp.zeros((lead, d_b), F32)
        hc_scr[...] = jnp.zeros_like(hc_scr)

    xn_scr[...] = _rms(x_ref[...], gm_ref[...]).astype(BF16)
    matmuls = []
    for k in range(wout_ref.shape[-1] // PROJ_CHUNK):
        def out_chunk(cs=pl.ds(k * PROJ_CHUNK, PROJ_CHUNK)):
            o_ref[:, cs] = xres_ref[:, cs] + jnp.dot(mg_scr[m_out], wout_ref[:, cs], preferred_element_type=F32)
        matmuls.append(out_chunk)
    for k in range(win_ref.shape[-1] // PROJ_CHUNK):
        def in_chunk(first=k * (PROJ_CHUNK // HEAD_DIM), cs=pl.ds(k * PROJ_CHUNK, PROJ_CHUNK)):
            res = jnp.dot(xn_scr[...], win_ref[:, cs], preferred_element_type=F32)
            for j in range(PROJ_CHUNK // HEAD_DIM):
                p_scr[p_new, first + j] = res[:, j * HEAD_DIM:(j + 1) * HEAD_DIM]
        matmuls.append(in_chunk)
    matmuls = iter(matmuls)

    def project(n_chunks):
        for _ in range(n_chunks):
            next(matmuls)()

    pcol = lambda col: p_scr[p_mix, col // HEAD_DIM]

    rr = lax.broadcasted_iota(jnp.int32, (CHUNK, CHUNK), 0)
    cc = lax.broadcasted_iota(jnp.int32, (CHUNK, CHUNK), 1)
    tril = (cc <= rr).astype(F32)
    for h in range(d_a // HEAD_DIM):
        project(MLP_HEAD_CHUNKS[h])
        cols = slice(h * HEAD_DIM, (h + 1) * HEAD_DIM)
        wsm = (ws_ref[h] * tril).astype(BF16)
        bcol = bs_ref[:, h:h + 1]
        for c in range(rows // CHUNK):
            rs = slice(c * CHUNK, (c + 1) * CHUNK)
            vh = _rms(p_scr[p_mix, (d_a + h * HEAD_DIM) // HEAD_DIM, rs, :], vg_ref[:, cols])
            if c == rows // CHUNK - 1:
                v_ref[:, cols] = vh
            z = jnp.dot(wsm, vh.astype(BF16), preferred_element_type=F32) + bcol
            ao_scr[rs, cols] = p_scr[p_mix, h, rs, :] * z
    mg_scr[m_mix, :, :d_a] = _rms(ao_scr[...], on_ref[:, :d_a]).astype(BF16)

    xc_scr[0:lead, :] = jnp.where(first_tile, 0.0, xc_scr[0:lead, :])
    for h in range(d_b // HEAD_DIM):
        cols = slice(h * HEAD_DIM, (h + 1) * HEAD_DIM)
        xc_scr[lead:lead + rows, cols] = pcol(2 * d_a + h * HEAD_DIM)
    conv_ref[...] = xc_scr[lead + rows - (CONV_W - 1):lead + rows, :]
    c_lam = -LRU_C * jax.nn.softplus(-lam_ref[...])
    blk = rows // LRU_ROW_SPLIT
    sub = lax.broadcasted_iota(jnp.int32, (blk // V7X_SUBLANES, V7X_SUBLANES, HEAD_DIM), 1)
    for h in range(d_b // HEAD_DIM):
        project(LRU_HEAD_CHUNKS[h])
        cols = slice(h * HEAD_DIM, (h + 1) * HEAD_DIM)
        for r0 in range(0, rows, blk):
            acc = cw_ref[0:1, cols] * xc_scr[lead - 3 + r0:lead - 3 + r0 + blk, cols]
            for k in range(1, CONV_W):
                acc = acc + cw_ref[k:k + 1, cols] * xc_scr[lead - 3 + k + r0:lead - 3 + k + r0 + blk, cols]
            xconv = cb_ref[:, cols] + acc
            a, bx = _lru_gates(xconv, wri_ref[h], br_ref[:, cols], bi_ref[:, cols], c_lam[:, cols])
            a3 = a.reshape(blk // V7X_SUBLANES, V7X_SUBLANES, HEAD_DIM)
            b3 = bx.reshape(blk // V7X_SUBLANES, V7X_SUBLANES, HEAD_DIM)
            for k in (1, 2, 4):
                a_prev = pltpu.roll(a3, k, axis=1)
                b_prev = pltpu.roll(b3, k, axis=1)
                keep = sub >= k
                b3 = jnp.where(keep, b3 + a3 * b_prev, b3)
                a3 = jnp.where(keep, a3 * a_prev, a3)
            a_scr[r0:r0 + blk, cols] = a3.reshape(blk, HEAD_DIM)
            b_scr[r0:r0 + blk, cols] = b3.reshape(blk, HEAD_DIM)
    xc_scr[0:lead, :] = xc_scr[rows:rows + lead, :]

    h_prev = jnp.where(first_tile, 0.0, hc_scr[...])
    for g in range(n_grp):
        gs = slice(g * V7X_SUBLANES, (g + 1) * V7X_SUBLANES)
        hs = b_scr[gs, :] + a_scr[gs, :] * h_prev
        b_scr[gs, :] = hs
        h_prev = jnp.broadcast_to(hs[V7X_SUBLANES - 1:, :], (V7X_SUBLANES, d_b))
    hc_scr[...] = h_prev
    h_ref[...] = h_prev[0:1, :]

    for h in range(d_b // HEAD_DIM):
        cols = slice(h * HEAD_DIM, (h + 1) * HEAD_DIM)
        ao_scr[:, cols] = b_scr[:, cols] * _gelu_tanh(pcol(2 * d_a + d_b + h * HEAD_DIM))
    mg_scr[m_mix, :, d_a:] = _rms(ao_scr[...], on_ref[:, d_a:]).astype(BF16)
    assert next(matmuls, None) is None


def _mix_sample_body(x_ref, gm_ref, win_ref, wout_ref, sc_ref, h0_ref, vg_ref, ws0_ref, bs0_ref, cw_ref,
                     cb_ref, wri_ref, br_ref, bi_ref, lam_ref, on_ref,
                     o_ref, conv_ref, h_ref, v_ref, winb_ref, woutb_ref,
                     xn_scr, p_scr, mg_scr, y_scr):
    d_a = vg_ref.shape[-1]
    d_b = wri_ref.shape[0] * HEAD_DIM

    def mix():
        for h in range(d_a // HEAD_DIM):
            cols = slice(h * HEAD_DIM, (h + 1) * HEAD_DIM)
            vh = _rms(_pcol(p_scr, d_a + h * HEAD_DIM), vg_ref[:, cols])
            v_ref[:, cols] = vh
            y_scr[:, cols] = _pcol(p_scr, h * HEAD_DIM) * (ws0_ref[:, cols] * vh + bs0_ref[:, cols])
        mg_scr[:, :d_a] = _rms(y_scr[...], on_ref[:, :d_a]).astype(BF16)

        for k in range(CONV_W - 2):
            conv_ref[k] = sc_ref[k + 1]
        c_lam = -LRU_C * jax.nn.softplus(-lam_ref[...])
        for h in range(d_b // HEAD_DIM):
            cols = slice(h * HEAD_DIM, (h + 1) * HEAD_DIM)
            xb = _pcol(p_scr, 2 * d_a + h * HEAD_DIM)
            conv_ref[CONV_W - 2, :, cols] = xb
            acc = cw_ref[0:1, cols] * sc_ref[0, :, cols]
            for k in range(1, CONV_W - 1):
                acc = acc + cw_ref[k:k + 1, cols] * sc_ref[k, :, cols]
            acc = acc + cw_ref[CONV_W - 1:CONV_W, cols] * xb
            xconv = cb_ref[:, cols] + acc
            a, bx = _lru_gates(xconv, wri_ref[h], br_ref[:, cols], bi_ref[:, cols], c_lam[:, cols])
            hs = a * h0_ref[:, cols] + bx
            h_ref[:, cols] = hs
            y_scr[:, cols] = hs * _gelu_tanh(_pcol(p_scr, 2 * d_a + d_b + h * HEAD_DIM))
        mg_scr[:, d_a:] = _rms(y_scr[...], on_ref[:, d_a:]).astype(BF16)

    _proj_steps(pl.program_id(0), p_scr.shape[0], x_ref, gm_ref, win_ref, wout_ref, o_ref, winb_ref, woutb_ref,
                xn_scr, p_scr, mg_scr, mix)


def _mixer_layer(x, batch, seq, layer, gm, w_in, w_out, sc_t, h0, vg, ws, bs_t, ws0, bs0, cw, cb, wri, br, bi, lam,
                 on):
    total_rows, d = x.shape
    d_in = w_in.shape[-1]
    n_a, n_b = ws.shape[1], wri.shape[1]
    d_a, d_b = n_a * HEAD_DIM, n_b * HEAD_DIM
    n_in, n_out = d_in // N_TILE, d // N_TILE
    steps = n_in + 1 + n_out
    tiles = seq // MIX_ROWS
    s_rows = total_rows - batch * seq
    s_blk = (batch * seq) // s_rows

    ix = lambda fn: (lambda *g: fn(g[-1]))
    ocol = lambda j: jnp.clip(j - n_in - 1, 0, n_out - 1)
    vec = lambda n: pl.BlockSpec((None, 1, n), ix(lambda j: (layer, 0, 0)))
    full = lambda *shape: pl.BlockSpec((None,) + shape, ix(lambda j: (layer,) + (0,) * len(shape)))
    win = pl.BlockSpec((None, d, N_TILE), ix(lambda j: (layer, 0, jnp.minimum(j, n_in - 1))))
    wout = pl.BlockSpec((None, d_a + d_b, N_TILE), ix(lambda j: (layer, 0, ocol(j))))

    x_new, conv_s, h_s, v_s, w_in_b, w_out_b = pl.pallas_call(
        _mix_sample_body,
        grid=(steps,),
        in_specs=[
            pl.BlockSpec((s_rows, d), lambda j: (s_blk, 0)),
            vec(d), win, wout,
            full(CONV_W - 1, s_rows, d_b), full(s_rows, d_b),
            vec(d_a), vec(d_a), vec(d_a), full(CONV_W, d_b), vec(d_b),
            full(n_b, HEAD_DIM, 2 * HEAD_DIM), vec(d_b), vec(d_b), vec(d_b), vec(d_a + d_b),
        ],
        out_specs=[
            pl.BlockSpec((s_rows, N_TILE), lambda j: (s_blk, ocol(j))),
            pl.BlockSpec((CONV_W - 1, s_rows, d_b), lambda j: (0, 0, 0)),
            pl.BlockSpec((s_rows, d_b), lambda j: (0, 0)),
            pl.BlockSpec((s_rows, d_a), lambda j: (0, 0)),
            pl.BlockSpec((d, N_TILE), lambda j: (0, jnp.minimum(j, n_in - 1))),
            pl.BlockSpec((d_a + d_b, N_TILE), lambda j: (0, ocol(j))),
        ],
        out_shape=[
            jax.ShapeDtypeStruct((total_rows, d), F32),
            jax.ShapeDtypeStruct((CONV_W - 1, s_rows, d_b), F32),
            jax.ShapeDtypeStruct((s_rows, d_b), F32),
            jax.ShapeDtypeStruct((s_rows, d_a), F32),
            jax.ShapeDtypeStruct((d, d_in), BF16),
            jax.ShapeDtypeStruct((d_a + d_b, d), BF16),
        ],
        scratch_shapes=[
            pltpu.VMEM((s_rows, d), BF16),
            pltpu.VMEM((n_in, s_rows, N_TILE), F32),
            pltpu.VMEM((s_rows, d_a + d_b), BF16),
            pltpu.VMEM((s_rows, d_a), F32),
        ],
        compiler_params=_params("arbitrary"),
        name="mix_sample",
    )(x, gm, w_in, w_out, sc_t, h0, vg, ws0, bs0, cw, cb, wri, br, bi, lam, on)

    n_tiles = batch * tiles
    mixed = lambda s: jnp.maximum(s - 1, 0) // tiles
    resident = lambda *shape: pl.BlockSpec(shape, lambda s: (0,) * len(shape))
    x_new, conv_p, h_p, v_p = pl.pallas_call(
        functools.partial(_mix_prompt_body, tiles),
        grid=(n_tiles + 2,),
        in_specs=[
            pl.BlockSpec((MIX_ROWS, d), lambda s: (jnp.minimum(s, n_tiles - 1), 0)),
            pl.BlockSpec((MIX_ROWS, d), lambda s: (jnp.maximum(s - 2, 0), 0)),
            pl.BlockSpec(memory_space=pl.ANY),
            vec(d), resident(d, d_in), resident(d_a + d_b, d), vec(d_a),
            full(n_a, CHUNK, CHUNK), full(CHUNK, n_a), full(CONV_W, d_b), vec(d_b),
            full(n_b, HEAD_DIM, 2 * HEAD_DIM), vec(d_b), vec(d_b), vec(d_b), vec(d_a + d_b),
        ],
        out_specs=[
            pl.BlockSpec((MIX_ROWS, d), lambda s: (jnp.maximum(s - 2, 0), 0)),
            pl.BlockSpec((None, CONV_W - 1, d_b), lambda s: (mixed(s), 0, 0)),
            pl.BlockSpec((None, 1, d_b), lambda s: (mixed(s), 0, 0)),
            pl.BlockSpec((None, CHUNK, d_a), lambda s: (mixed(s), 0, 0)),
        ],
        out_shape=[
            jax.ShapeDtypeStruct((total_rows, d), F32),
            jax.ShapeDtypeStruct((batch + 1, CONV_W - 1, d_b), F32),
            jax.ShapeDtypeStruct((batch + 1, 1, d_b), F32),
            jax.ShapeDtypeStruct((batch + 1, CHUNK, d_a), F32),
        ],
        scratch_shapes=[
            pltpu.VMEM((2, d_in // HEAD_DIM, MIX_ROWS, HEAD_DIM), F32),
            pltpu.VMEM((MIX_ROWS, d), BF16),
            pltpu.VMEM((2, MIX_ROWS, d_a + d_b), BF16),
            pltpu.VMEM((MIX_ROWS + 2 * V7X_SUBLANES, d_b), F32),
            pltpu.VMEM((MIX_ROWS, d_b), F32),
            pltpu.VMEM((MIX_ROWS, d_b), F32),
            pltpu.VMEM((MIX_ROWS, d_a), F32),
            pltpu.VMEM((V7X_SUBLANES, d_b), F32),
        ],
        input_output_aliases={2: 0},
        compiler_params=_params("arbitrary"),
        name="mix_prompt",
    )(x, x, x_new, gm, w_in_b, w_out_b, vg, ws, bs_t, cw, cb, wri, br, bi, lam, on)

    return x_new, conv_p[:batch], h_p[:batch], v_p[:batch], conv_s, h_s, v_s


def kernel(x_prompt, x_sample, state_conv, state_h, ffn1_norm, ffn1_wg, ffn1_wu, ffn1_wd, mix_norm, w_in, v_norm,
           w_spatial, b_spatial, conv_w, conv_b, w_rgate, b_rgate, w_igate, b_igate, lru_lambda, out_norm, w_out,
           ffn2_norm, ffn2_wg, ffn2_wu, ffn2_wd, final_norm):
    batch, seq, d_model = x_prompt.shape
    dec_batch, dec_seq, _ = x_sample.shape
    depth = w_in.shape[0]
    assert dec_seq == 1 and seq % MIX_ROWS == 0 and MIX_ROWS % CHUNK == 0 and (batch * seq) % NORM_ROWS == 0
    p_rows = batch * seq
    rows = p_rows + dec_batch
    assert rows % (ROW_TILES * 2 * V7X_SUBLANES) == 0 and p_rows % dec_batch == 0

    x = jnp.concatenate([x_prompt.reshape(p_rows, d_model), x_sample.reshape(dec_batch, d_model)], axis=0)

    row3 = lambda a: a.reshape(a.shape[0], 1, a.shape[-1])
    bf = lambda a: a.astype(BF16)
    g1, gm, g2 = row3(ffn1_norm), row3(mix_norm), row3(ffn2_norm)
    wri = bf(jnp.concatenate([w_rgate, w_igate], axis=-1))
    vg, cb, br, bi, lam, on = (row3(a) for a in (v_norm, conv_b, b_rgate, b_igate, lru_lambda, out_norm))
    bs_t = jnp.transpose(b_spatial, (0, 2, 1))
    ws0 = row3(jnp.repeat(w_spatial[:, :, 0, 0], HEAD_DIM, axis=-1))
    bs0 = row3(jnp.repeat(b_spatial[:, :, 0], HEAD_DIM, axis=-1))
    sc_t = jnp.transpose(state_conv, (0, 2, 1, 3))

    conv_p, h_p, v_p, conv_s, h_s, v_s = [], [], [], [], [], []
    for l in range(depth):
        x = _ffn(x, g1, ffn1_wg, ffn1_wu, ffn1_wd, l)
        x, cp, hp, vp, cs, hs, vs = _mixer_layer(x, batch, seq, l, gm, w_in, w_out, sc_t, state_h, vg,
                                                 w_spatial, bs_t, ws0, bs0, conv_w, cb, wri, br, bi, lam, on)
        x = _ffn(x, g2, ffn2_wg, ffn2_wu, ffn2_wd, l)
        conv_p.append(cp)
        h_p.append(hp.reshape(batch, -1))
        v_p.append(vp)
        conv_s.append(jnp.transpose(cs, (1, 0, 2)))
        h_s.append(hs)
        v_s.append(vs.reshape(dec_batch, dec_seq, -1))

    gf = final_norm.reshape(1, d_model)
    y_prompt = _final_norm(x, gf, NORM_ROWS, 0, p_rows // NORM_ROWS).reshape(batch, seq, d_model)
    y_sample = _final_norm(x, gf, dec_batch, p_rows // dec_batch, 1).reshape(dec_batch, dec_seq, d_model)
    return (y_prompt, y_sample, jnp.stack(conv_p), jnp.stack(h_p), jnp.stack(v_p),
            jnp.stack(conv_s), jnp.stack(h_s), jnp.stack(v_s))
```
